```python
import math
import jax, jax.numpy as jnp
from jax import lax
import numpy as np

D_MODEL = 1024
BATCH = 2
SEQ = 16384
DEPTH = 2

N_A_LAYERS = DEPTH // 2
N_B_LAYERS = DEPTH - N_A_LAYERS

A_HEADS = 4
A_QK_DIM = D_MODEL // 2 // A_HEADS
A_V_DIM = D_MODEL // A_HEADS
A_CHUNK = 64
GATE_SOFTCAP = 15.0
A_IN_DIM = 2 * A_HEADS * A_QK_DIM + 2 * A_HEADS * A_V_DIM + 2 * A_HEADS

B_GROUPS = ((128, 1), (512, 4), (2048, 16))
N_GROUPS = len(B_GROUPS)
B_HEAD_DIM = 64
B_HEADS = D_MODEL // B_HEAD_DIM
B_BLOCK = 128
REL_BUCKETS = 32
REL_MAX_DIST = 2048

D_FF = 2816
CONV_WIDTH = 3
EPS = 1e-6

kernel_name = "yoco_mlstm_dilated_attn_convffn"


def rmsnorm(x, g):
    xf = x.astype(jnp.float32)
    y = xf * lax.rsqrt(jnp.mean(xf * xf, axis=-1, keepdims=True) + EPS)
    return (y * g.astype(jnp.float32)).astype(x.dtype)


def softcap(z):
    return GATE_SOFTCAP * jnp.tanh(z / GATE_SOFTCAP)


def mlstm_chunkwise(q, k, v, li, lf):
    Bsz, S, H, dk = q.shape
    dv = v.shape[-1]
    L = A_CHUNK
    NC = S // L

    def to_chunks(t):
        return t.reshape(Bsz, NC, L, H, t.shape[-1]).transpose(1, 0, 3, 2, 4)

    def gate_chunks(t):
        return t.reshape(Bsz, NC, L, H).transpose(1, 0, 3, 2)

    causal = jnp.tril(jnp.ones((L, L), dtype=bool))

    def step(carry, xs):
        C, n, m = carry
        qc, kc, vc, lic, lfc = xs
        b = jnp.cumsum(lfc, axis=-1)
        logD = b[..., :, None] - b[..., None, :] + lic[..., None, :]
        logD = jnp.where(causal, logD, -jnp.inf)
        m_inter = b + m[..., None]
        m_t = jnp.maximum(m_inter, jnp.max(logD, axis=-1))
        Sm = jnp.einsum('bhtd,bhsd->bhts', qc, kc) * jnp.exp(logD - m_t[..., None])
        w_inter = jnp.exp(m_inter - m_t)
        num = (jnp.einsum('bhts,bhsv->bhtv', Sm, vc)
               + w_inter[..., None] * jnp.einsum('bhtd,bhdv->bhtv', qc, C))
        den = jnp.sum(Sm, axis=-1) + w_inter * jnp.einsum('bhtd,bhd->bht', qc, n)
        h = num / jnp.maximum(jnp.abs(den), jnp.exp(-m_t))[..., None]
        bL = b[..., -1]
        g = bL[..., None] - b + lic
        m_new = jnp.maximum(bL + m, jnp.max(g, axis=-1))
        wk = jnp.exp(g - m_new[..., None])
        decay = jnp.exp(bL + m - m_new)
        C_new = decay[..., None, None] * C + jnp.einsum('bhs,bhsd,bhsv->bhdv', wk, kc, vc)
        n_new = decay[..., None] * n + jnp.einsum('bhs,bhsd->bhd', wk, kc)
        return (C_new, n_new, m_new), h

    init = (jnp.zeros((Bsz, H, dk, dv), jnp.float32),
            jnp.zeros((Bsz, H, dk), jnp.float32),
            jnp.zeros((Bsz, H), jnp.float32))
    _, hs = lax.scan(step, init, (to_chunks(q), to_chunks(k), to_chunks(v),
                                  gate_chunks(li), gate_chunks(lf)))
    return hs.transpose(1, 0, 3, 2, 4).reshape(Bsz, S, H, dv)


def mlstm_layer(x, norm_g, w_in, b_if, hnorm_g, w_out):
    Bsz, S, _ = x.shape
    nq = A_HEADS * A_QK_DIM
    nv = A_HEADS * A_V_DIM
    z = rmsnorm(x, norm_g) @ w_in
    q, k, v, o, gi, gf = jnp.split(
        z, [nq, 2 * nq, 2 * nq + nv, 2 * nq + 2 * nv, 2 * nq + 2 * nv + A_HEADS], axis=-1)
    q = q.reshape(Bsz, S, A_HEADS, A_QK_DIM).astype(jnp.float32) * (A_QK_DIM ** -0.5)
    k = k.reshape(Bsz, S, A_HEADS, A_QK_DIM).astype(jnp.float32)
    v = v.reshape(Bsz, S, A_HEADS, A_V_DIM).astype(jnp.float32)
    bf = b_if.astype(jnp.float32)
    li = softcap(gi.astype(jnp.float32) + bf[:A_HEADS])
    lf = jax.nn.log_sigmoid(softcap(gf.astype(jnp.float32) + bf[A_HEADS:]))
    h = mlstm_chunkwise(q, k, v, li, lf)
    h = h * lax.rsqrt(jnp.mean(h * h, axis=-1, keepdims=True) + EPS) * hnorm_g.astype(jnp.float32)
    h = h.astype(x.dtype) * jax.nn.sigmoid(o.reshape(Bsz, S, A_HEADS, A_V_DIM))
    return h.reshape(Bsz, S, nv) @ w_out


def conv_ffn(x, norm_g, w_up, conv_w, conv_b, w_down):
    u = rmsnorm(x, norm_g) @ w_up
    up = jnp.pad(u, ((0, 0), (CONV_WIDTH - 1, 0), (0, 0)))
    u = up[:, :-2] * conv_w[0] + up[:, 1:-1] * conv_w[1] + up[:, 2:] * conv_w[2] + conv_b
    gate, val = jnp.split(u, 2, axis=-1)
    return (jax.nn.silu(gate) * val) @ w_down


def t5_bucket(dist):
    max_exact = REL_BUCKETS // 2
    d = np.maximum(dist, 0)
    log_ratio = np.log(np.maximum(d, 1) / max_exact) / math.log(REL_MAX_DIST / max_exact)
    large = np.minimum(max_exact + (log_ratio * (REL_BUCKETS - max_exact)).astype(np.int64),
                       REL_BUCKETS - 1)
    return np.where(d < max_exact, d, large).astype(np.int32)


def dilated_group(q, k, v, bias, win, dil):
    Bsz, S_pad, H, dh = q.shape
    nb = S_pad // (dil * B_BLOCK)

    def blocks(t):
        return t.reshape(Bsz, nb, B_BLOCK, dil, H, dh)

    def with_prev(t):
        prev = jnp.pad(t[:, :-1], ((0, 0), (1, 0), (0, 0), (0, 0), (0, 0), (0, 0)))
        return jnp.concatenate([prev, t], axis=2)

    qb = blocks(q)
    kc, vc = with_prev(blocks(k)), with_prev(blocks(v))
    delta = B_BLOCK + np.arange(B_BLOCK)[:, None] - np.arange(2 * B_BLOCK)[None, :]
    band = (delta >= 0) & (delta <= win // dil)
    first = (np.arange(nb)[:, None, None] == 0) & (np.arange(2 * B_BLOCK)[None, None, :] < B_BLOCK)
    valid = jnp.asarray(band[None] & ~first)[None, :, None, None]
    s = jnp.einsum('bnqrhe,bnkrhe->bnrhqk', qb, kc).astype(jnp.float32) * (dh ** -0.5) + bias
    s = jnp.where(valid, s, -jnp.inf)
    m = jnp.max(s, axis=-1, keepdims=True)
    p = jnp.exp(s - m)
    l = jnp.sum(p, axis=-1)
    o = jnp.einsum('bnrhqk,bnkrhe->bnrhqe', p.astype(vc.dtype), vc).astype(jnp.float32) / l[..., None]
    lse = m[..., 0] + jnp.log(l)
    o = o.transpose(0, 1, 4, 2, 3, 5).reshape(Bsz, S_pad, H, dh)
    lse = lse.transpose(0, 1, 4, 2, 3).reshape(Bsz, S_pad, H)
    return o, lse


def dilated_attention_layer(x, k_sh, v_sh, norm_g, w_q, w_out, rel_bias):
    Bsz, S, _ = x.shape
    q = (rmsnorm(x, norm_g) @ w_q).reshape(Bsz, S, N_GROUPS, B_HEADS, B_HEAD_DIM)
    span = max(d for _, d in B_GROUPS) * B_BLOCK
    S_pad = -(-S // span) * span
    padw = ((0, 0), (0, S_pad - S), (0, 0), (0, 0), (0, 0))
    q, k, v = jnp.pad(q, padw), jnp.pad(k_sh, padw), jnp.pad(v_sh, padw)
    outs, lses = [], []
    for g, (win, dil) in enumerate(B_GROUPS):
        delta = B_BLOCK + np.arange(B_BLOCK)[:, None] - np.arange(2 * B_BLOCK)[None, :]
        bucket = t5_bucket(delta * dil)
        bias = rel_bias[bucket][..., g * B_HEADS:(g + 1) * B_HEADS]
        bias = bias.transpose(2, 0, 1).astype(jnp.float32)
        o, lse = dilated_group(q[:, :, g], k[:, :, g], v[:, :, g], bias, win, dil)
        outs.append(o)
        lses.append(lse)
    w = jax.nn.softmax(jnp.stack(lses), axis=0)
    out = jnp.sum(w[..., None] * jnp.stack(outs), axis=0)[:, :S]
    return out.astype(x.dtype).reshape(Bsz, S, B_HEADS * B_HEAD_DIM) @ w_out


def shared_kv(x, kv_norm_g, w_kv):
    Bsz, S, _ = x.shape
    kv = (rmsnorm(x, kv_norm_g) @ w_kv).reshape(Bsz, S, 2, N_GROUPS, B_HEADS, B_HEAD_DIM)
    return kv[:, :, 0], kv[:, :, 1]


def setup_inputs(seed: int = 0) -> dict:
    key = jax.random.key(seed)
    ks = jax.random.split(key, 20)
    f32 = jnp.float32
    nrm = lambda k, s, sc: jax.random.normal(k, s, f32) * sc
    att_w = N_GROUPS * B_HEADS * B_HEAD_DIM
    return {
        "x": nrm(ks[0], (BATCH, SEQ, D_MODEL), 1.0),
        "a_norm_g": 1.0 + nrm(ks[1], (N_A_LAYERS, D_MODEL), 0.02),
        "a_w_in": nrm(ks[2], (N_A_LAYERS, D_MODEL, A_IN_DIM), D_MODEL ** -0.5),
        "a_b_if": jnp.concatenate([nrm(ks[3], (N_A_LAYERS, A_HEADS), 0.1),
                                   3.0 + 3.0 * jax.random.uniform(ks[4], (N_A_LAYERS, A_HEADS), f32)], axis=-1),
        "a_hnorm_g": 1.0 + nrm(ks[5], (N_A_LAYERS, A_HEADS, A_V_DIM), 0.02),
        "a_w_out": nrm(ks[6], (N_A_LAYERS, A_HEADS * A_V_DIM, D_MODEL), (A_HEADS * A_V_DIM) ** -0.5),
        "kv_norm_g": 1.0 + nrm(ks[7], (D_MODEL,), 0.02),
        "w_kv": nrm(ks[8], (D_MODEL, 2 * att_w), D_MODEL ** -0.5),
        "b_norm_g": 1.0 + nrm(ks[9], (N_B_LAYERS, D_MODEL), 0.02),
        "b_w_q": nrm(ks[10], (N_B_LAYERS, D_MODEL, att_w), D_MODEL ** -0.5),
        "b_w_out": nrm(ks[11], (N_B_LAYERS, B_HEADS * B_HEAD_DIM, D_MODEL), (B_HEADS * B_HEAD_DIM) ** -0.5),
        "rel_bias": nrm(ks[12], (REL_BUCKETS, N_GROUPS * B_HEADS), 0.5),
        "f_norm_g": 1.0 + nrm(ks[13], (DEPTH, D_MODEL), 0.02),
        "f_w_up": nrm(ks[14], (DEPTH, D_MODEL, 2 * D_FF), D_MODEL ** -0.5),
        "f_conv_w": nrm(ks[15], (DEPTH, CONV_WIDTH, 2 * D_FF), CONV_WIDTH ** -0.5),
        "f_conv_b": nrm(ks[16], (DEPTH, 2 * D_FF), 0.01),
        "f_w_down": nrm(ks[17], (DEPTH, D_FF, D_MODEL), D_FF ** -0.5),
        "final_norm_g": 1.0 + nrm(ks[18], (D_MODEL,), 0.02),
    }


def reference(x, a_norm_g, a_w_in, a_b_if, a_hnorm_g, a_w_out, kv_norm_g, w_kv,
              b_norm_g, b_w_q, b_w_out, rel_bias, f_norm_g, f_w_up, f_conv_w,
              f_conv_b, f_w_down, final_norm_g):
    k_sh = v_sh = None
    for layer in range(DEPTH):
        if layer < N_A_LAYERS:
            x = x + mlstm_layer(x, a_norm_g[layer], a_w_in[layer], a_b_if[layer],
                                a_hnorm_g[layer], a_w_out[layer])
        else:
            j = layer - N_A_LAYERS
            if j == 0:
                k_sh, v_sh = shared_kv(x, kv_norm_g, w_kv)
            x = x + dilated_attention_layer(x, k_sh, v_sh, b_norm_g[j], b_w_q[j],
                                            b_w_out[j], rel_bias)
        x = x + conv_ffn(x, f_norm_g[layer], f_w_up[layer], f_conv_w[layer],
                         f_conv_b[layer], f_w_down[layer])
    return rmsnorm(x, final_norm_g)
```

```python
import functools
import math

import numpy as np
import jax
import jax.numpy as jnp
from jax import lax
from jax.experimental import pallas as pl
from jax.experimental.pallas import tpu as pltpu

F32 = jnp.float32
BF16 = jnp.bfloat16

D_MODEL = 1024
A_HEADS = 4
A_QK_DIM = 128
A_V_DIM = 256
A_NQ = A_HEADS * A_QK_DIM
A_NV = A_HEADS * A_V_DIM
A_Z_DIM = 2 * A_NQ + 2 * A_NV
GATE_SOFTCAP = 15.0
B_GROUPS = ((128, 1), (512, 4), (2048, 16))
B_HEAD_DIM = 64
B_HEADS = 16
B_BLOCK = 128
REL_BUCKETS = 32
REL_MAX_DIST = 2048
D_FF = 2816
EPS = 1e-6
MASK_VALUE = -1e30

GATE_LANES = 128
GATE_ROWS = 16
MLSTM_CHUNK = 256
FFN_ROWS = 512
FFN_COLS = 256
CONV_HALO = 8
ATT_ROWS = 512
MERGE_ROWS = 512
VMEM_LIMIT = 56 * 1024 * 1024


def _rms_scale(x):
    return lax.rsqrt(jnp.mean(x * x, axis=-1, keepdims=True) + EPS)


def _softcap(z):
    return GATE_SOFTCAP * jnp.tanh(z / GATE_SOFTCAP)


def _log_sigmoid(a):
    return jnp.minimum(a, 0.0) - jnp.log1p(jnp.exp(-jnp.abs(a)))


def _split3(v):
    hi = v.astype(BF16)
    r1 = v - hi.astype(F32)
    mid = r1.astype(BF16)
    lo = (r1 - mid.astype(F32)).astype(BF16)
    return hi, mid, lo


def _dot(a, b):
    return jnp.dot(a, b, preferred_element_type=F32)


def _dot_nt(a, b):
    return lax.dot_general(a, b, (((1,), (1,)), ((), ())), preferred_element_type=F32)


def _dot_tn(a, b):
    return lax.dot_general(a, b, (((0,), (0,)), ((), ())), preferred_element_type=F32)


def _mlstm_kernel(x_ref, g_ref, wz_ref, wgc_ref, wgr_ref, bc_ref, br_ref, hg_ref, wo_ref,
                  out_ref, c_ref, n_ref, m_ref, hcat_ref):
    L = x_ref.shape[0]

    @pl.when(pl.program_id(1) == 0)
    def _():
        c_ref[...] = jnp.zeros_like(c_ref)
        n_ref[...] = jnp.zeros_like(n_ref)
        m_ref[...] = jnp.zeros_like(m_ref)

    x = x_ref[...]
    xn = (x * _rms_scale(x) * g_ref[...]).astype(BF16)
    z = _dot(xn, wz_ref[...])
    ac = _softcap(_dot(xn, wgc_ref[...]) + bc_ref[...])
    ar = _softcap(_dot_nt(wgr_ref[...], xn) + br_ref[...])
    lfc = _log_sigmoid(ac)
    lfr = _log_sigmoid(ar)

    row = lax.broadcasted_iota(jnp.int32, (L, L), 0)
    col = lax.broadcasted_iota(jnp.int32, (L, L), 1)
    causal = col <= row
    tril = jnp.where(causal, 1.0, 0.0).astype(BF16)
    triu = jnp.where(row <= col, 1.0, 0.0).astype(BF16)
    bcs = sum(_dot(tril, p) for p in _split3(lfc))
    brs = sum(_dot(p, triu) for p in _split3(lfr))

    for h in range(A_HEADS):
        q = z[:, h * A_QK_DIM:(h + 1) * A_QK_DIM] * (A_QK_DIM ** -0.5)
        k = z[:, A_NQ + h * A_QK_DIM:A_NQ + (h + 1) * A_QK_DIM]
        v = z[:, 2 * A_NQ + h * A_V_DIM:2 * A_NQ + (h + 1) * A_V_DIM]
        o = z[:, 2 * A_NQ + A_NV + h * A_V_DIM:2 * A_NQ + A_NV + (h + 1) * A_V_DIM]
        qb, kb, vb = q.astype(BF16), k.astype(BF16), v.astype(BF16)
        li_c = ac[:, h:h + 1]
        li_r = ar[h:h + 1, :]
        b_c = bcs[:, A_HEADS + h:A_HEADS + h + 1]
        b_r = brs[A_HEADS + h:A_HEADS + h + 1, :]
        m_prev = m_ref[h]
        c_prev = c_ref[h]
        n_prev = n_ref[h]

        log_d = jnp.where(causal, b_c - b_r + li_r, -jnp.inf)
        m_inter = b_c + m_prev
        m_t = jnp.maximum(m_inter, jnp.max(log_d, axis=-1, keepdims=True))
        sm = _dot_nt(qb, kb) * jnp.exp(log_d - m_t)
        w_inter = jnp.exp(m_inter - m_t)
        num = _dot(sm.astype(BF16), vb) + w_inter * _dot(qb, c_prev.astype(BF16))
        den = (jnp.sum(sm, axis=-1, keepdims=True)
               + w_inter * jnp.sum(q * n_prev, axis=-1, keepdims=True))
        hv = num * (1.0 / jnp.maximum(jnp.abs(den), jnp.exp(-m_t)))
        hv = hv * _rms_scale(hv) * hg_ref[:, h * A_V_DIM:(h + 1) * A_V_DIM]
        hcat_ref[:, h * A_V_DIM:(h + 1) * A_V_DIM] = (hv * jax.nn.sigmoid(o)).astype(BF16)

        b_last = b_c[L - 1:L, :]
        g_c = b_last - b_c + li_c
        g_r = b_last - b_r + li_r
        m_new = jnp.maximum(b_last + m_prev, jnp.max(g_r, axis=-1, keepdims=True))
        decay = jnp.exp(b_last + m_prev - m_new)
        kw = k * jnp.exp(g_c - m_new)
        c_ref[h] = decay * c_prev + _dot_tn(kw.astype(BF16), vb)
        n_ref[h] = decay * n_prev + jnp.sum(kw, axis=0, keepdims=True)
        m_ref[h] = m_new

    out_ref[...] = x + _dot(hcat_ref[...], wo_ref[...])


def _const_spec(shape):
    return pl.BlockSpec(shape, lambda *_: (0,) * len(shape), pipeline_mode=pl.Buffered(1))


def _mlstm_layer(x, norm_g, w_in, b_if, hnorm_g, w_out):
    B, S, D = x.shape
    L = MLSTM_CHUNK
    wz = w_in[:, :A_Z_DIM].astype(BF16)
    wg = w_in[:, A_Z_DIM:]
    ng = 2 * A_HEADS
    wgc = jnp.pad(wg, ((0, 0), (0, GATE_LANES - ng))).astype(BF16)
    wgr = jnp.pad(wg.T, ((0, GATE_ROWS - ng), (0, 0))).astype(BF16)
    bc = jnp.pad(b_if[None, :], ((0, 0), (0, GATE_LANES - ng)))
    br = jnp.pad(b_if[:, None], ((0, GATE_ROWS - ng), (0, 0)))
    row_spec = pl.BlockSpec((None, L, D), lambda b, c: (b, c, 0))
    return pl.pallas_call(
        _mlstm_kernel,
        out_shape=jax.ShapeDtypeStruct((B, S, D), F32),
        grid=(B, S // L),
        in_specs=[row_spec, _const_spec((1, D)), _const_spec((D, A_Z_DIM)),
                  _const_spec((D, GATE_LANES)), _const_spec((GATE_ROWS, D)),
                  _const_spec((1, GATE_LANES)), _const_spec((GATE_ROWS, 1)),
                  _const_spec((1, A_NV)), _const_spec((A_NV, D))],
        out_specs=row_spec,
        scratch_shapes=[pltpu.VMEM((A_HEADS, A_QK_DIM, A_V_DIM), F32),
                        pltpu.VMEM((A_HEADS, 1, A_QK_DIM), F32),
                        pltpu.VMEM((A_HEADS, 1, 1), F32),
                        pltpu.VMEM((L, A_NV), BF16)],
        compiler_params=pltpu.CompilerParams(
            dimension_semantics=("arbitrary", "arbitrary"), vmem_limit_bytes=VMEM_LIMIT),
        name="mlstm_layer",
    )(x, norm_g[None, :], wz, wgc, wgr, bc, br, hnorm_g.reshape(1, A_NV), w_out.astype(BF16))


def _ffn_kernel(x_ref, g_ref, wup_ref, cw_ref, cb_ref, wdn_ref, fg_ref, out_ref,
                ubuf_ref, act_ref, *, final_norm):
    T = x_ref.shape[0]
    H = CONV_HALO

    @pl.when(pl.program_id(1) == 0)
    def _():
        ubuf_ref[0:H, :] = jnp.zeros((H, 2 * D_FF), F32)

    x = x_ref[...]
    xn = (x * _rms_scale(x) * g_ref[...]).astype(BF16)
    cw = 2 * FFN_COLS
    for j in range(2 * D_FF // cw):
        ubuf_ref[H:H + T, j * cw:(j + 1) * cw] = _dot(xn, wup_ref[:, j * cw:(j + 1) * cw])

    def conv(c0):
        cs = slice(c0, c0 + FFN_COLS)
        return (ubuf_ref[H - 2:H - 2 + T, cs] * cw_ref[0:1, cs]
                + ubuf_ref[H - 1:H - 1 + T, cs] * cw_ref[1:2, cs]
                + ubuf_ref[H:H + T, cs] * cw_ref[2:3, cs] + cb_ref[:, cs])

    for j in range(D_FF // FFN_COLS):
        gate = conv(j * FFN_COLS)
        val = conv(D_FF + j * FFN_COLS)
        act_ref[:, j * FFN_COLS:(j + 1) * FFN_COLS] = (gate * jax.nn.sigmoid(gate) * val).astype(BF16)

    ubuf_ref[0:H, :] = ubuf_ref[T:T + H, :]
    y = x + _dot(act_ref[...], wdn_ref[...])
    if final_norm:
        y = y * _rms_scale(y) * fg_ref[...]
    out_ref[...] = y


def _conv_ffn(x, norm_g, w_up, conv_w, conv_b, w_down, final_g, final_norm):
    B, S, D = x.shape
    T = FFN_ROWS
    row_spec = pl.BlockSpec((None, T, D), lambda b, t: (b, t, 0))
    return pl.pallas_call(
        functools.partial(_ffn_kernel, final_norm=final_norm),
        out_shape=jax.ShapeDtypeStruct((B, S, D), F32),
        grid=(B, S // T),
        in_specs=[row_spec, _const_spec((1, D)), _const_spec((D, 2 * D_FF)),
                  _const_spec((3, 2 * D_FF)), _const_spec((1, 2 * D_FF)),
                  _const_spec((D_FF, D)), _const_spec((1, D))],
        out_specs=row_spec,
        scratch_shapes=[pltpu.VMEM((T + CONV_HALO, 2 * D_FF), F32),
                        pltpu.VMEM((T, D_FF), BF16)],
        compiler_params=pltpu.CompilerParams(
            dimension_semantics=("arbitrary", "arbitrary"), vmem_limit_bytes=VMEM_LIMIT),
        name="conv_ffn_final" if final_norm else "conv_ffn",
    )(x, norm_g[None, :], w_up.astype(BF16), conv_w, conv_b[None, :], w_down.astype(BF16),
      final_g[None, :])


def _attn_kernel(x_ref, gq_ref, gkv_ref, wq_ref, wk_ref, wv_ref, bias_ref, o_ref, lse_ref,
                 qe_ref, qo_ref, k_ref, v_ref):
    T = x_ref.shape[0]
    P = B_BLOCK
    i = pl.program_id(2)

    @pl.when(i == 0)
    def _():
        k_ref[0:P, :] = jnp.zeros((P, D_MODEL), BF16)
        v_ref[0:P, :] = jnp.zeros((P, D_MODEL), BF16)

    @pl.when(i > 0)
    def _():
        k_ref[0:P, :] = k_ref[T:T + P, :]
        v_ref[0:P, :] = v_ref[T:T + P, :]

    x = x_ref[...]
    xr = x * _rms_scale(x)
    xq = (xr * gq_ref[...]).astype(BF16)
    xkv = (xr * gkv_ref[...]).astype(BF16)
    lane = lax.broadcasted_iota(jnp.int32, (T, D_MODEL), 1)
    even_head = (lane % (2 * B_HEAD_DIM)) < B_HEAD_DIM
    q = _dot(xq, wq_ref[...]) * (B_HEAD_DIM ** -0.5)
    qe_ref[...] = jnp.where(even_head, q, 0.0).astype(BF16)
    qo_ref[...] = jnp.where(even_head, 0.0, q).astype(BF16)
    k_ref[P:P + T, :] = _dot(xkv, wk_ref[...]).astype(BF16)
    v_ref[P:P + T, :] = _dot(xkv, wv_ref[...]).astype(BF16)

    lane2 = lax.broadcasted_iota(jnp.int32, (P, 2 * B_HEAD_DIM), 1)
    first_half = lane2 < B_HEAD_DIM
    kcol = lax.broadcasted_iota(jnp.int32, (1, 2 * P), 1)

    def qblock(j, carry):
        r0 = pl.multiple_of(j * P, P)
        no_prev = jnp.where((kcol < P) & (i == 0) & (j == 0), MASK_VALUE, 0.0)
        for hp in range(B_HEADS // 2):
            cs = slice(hp * 2 * B_HEAD_DIM, (hp + 1) * 2 * B_HEAD_DIM)
            k2 = k_ref[pl.ds(r0, 2 * P), cs]
            v2 = v_ref[pl.ds(r0, 2 * P), cs]
            outs, lses = [], []
            for half, q_ref in enumerate((qe_ref, qo_ref)):
                s = _dot_nt(q_ref[pl.ds(r0, P), cs], k2) + bias_ref[2 * hp + half] + no_prev
                m = jnp.max(s, axis=-1, keepdims=True)
                p = jnp.exp(s - m)
                l = jnp.sum(p, axis=-1, keepdims=True)
                outs.append(_dot(p.astype(BF16), v2) * (1.0 / l))
                lses.append(m + jnp.log(l))
            o_ref[pl.ds(r0, P), cs] = jnp.where(first_half, outs[0], outs[1])
            lse_ref[pl.ds(r0, P), cs] = jnp.where(first_half, lses[0], lses[1])
        return carry

    lax.fori_loop(0, T // P, qblock, 0)


def _t5_bucket(dist):
    max_exact = REL_BUCKETS // 2
    d = np.maximum(dist, 0)
    log_ratio = np.log(np.maximum(d, 1) / max_exact) / math.log(REL_MAX_DIST / max_exact)
    large = np.minimum(max_exact + (log_ratio * (REL_BUCKETS - max_exact)).astype(np.int64),
                       REL_BUCKETS - 1)
    return np.where(d < max_exact, d, large).astype(np.int32)


def _attn_group(x, g, win, dil, gq, gkv, wq, wk, wv, rel_bias):
    B, S, D = x.shape
    T = ATT_ROWS
    n = S // dil
    delta = B_BLOCK + np.arange(B_BLOCK)[:, None] - np.arange(2 * B_BLOCK)[None, :]
    band = (delta >= 0) & (delta <= win // dil)
    bias = rel_bias[_t5_bucket(delta * dil)][..., g * B_HEADS:(g + 1) * B_HEADS]
    bias = jnp.where(band[None], bias.transpose(2, 0, 1).astype(F32), MASK_VALUE)
    xv = x.reshape(B, n, dil * D)
    cls_spec = pl.BlockSpec((None, T, D), lambda b, r, i: (b, i, r))
    o, lse = pl.pallas_call(
        _attn_kernel,
        out_shape=[jax.ShapeDtypeStruct((B, n, dil * D), F32)] * 2,
        grid=(B, dil, n // T),
        in_specs=[cls_spec, _const_spec((1, D)), _const_spec((1, D)),
                  _const_spec((D, D)), _const_spec((D, D)), _const_spec((D, D)),
                  _const_spec((B_HEADS, B_BLOCK, 2 * B_BLOCK))],
        out_specs=[cls_spec, cls_spec],
        scratch_shapes=[pltpu.VMEM((T, D), BF16), pltpu.VMEM((T, D), BF16),
                        pltpu.VMEM((T + B_BLOCK, D), BF16), pltpu.VMEM((T + B_BLOCK, D), BF16)],
        compiler_params=pltpu.CompilerParams(
            dimension_semantics=("arbitrary", "arbitrary", "arbitrary"),
            vmem_limit_bytes=VMEM_LIMIT),
        name=f"dilated_attn_g{g}",
    )(xv, gq[None, :], gkv[None, :], wq.astype(BF16), wk.astype(BF16), wv.astype(BF16), bias)
    return o.reshape(B, S, D), lse.reshape(B, S, D)


def _merge_kernel(x_ref, o0_ref, o1_ref, o2_ref, l0_ref, l1_ref, l2_ref, wo_ref, out_ref):
    l0, l1, l2 = l0_ref[...], l1_ref[...], l2_ref[...]
    m = jnp.maximum(jnp.maximum(l0, l1), l2)
    e0, e1, e2 = jnp.exp(l0 - m), jnp.exp(l1 - m), jnp.exp(l2 - m)
    merged = (e0 * o0_ref[...] + e1 * o1_ref[...] + e2 * o2_ref[...]) * (1.0 / (e0 + e1 + e2))
    out_ref[...] = x_ref[...] + _dot(merged.astype(BF16), wo_ref[...])


def _merge_groups(x, outs, lses, w_out):
    B, S, D = x.shape
    T = MERGE_ROWS
    row_spec = pl.BlockSpec((None, T, D), lambda b, t: (b, t, 0))
    return pl.pallas_call(
        _merge_kernel,
        out_shape=jax.ShapeDtypeStruct((B, S, D), F32),
        grid=(B, S // T),
        in_specs=[row_spec] * 7 + [_const_spec((D, D))],
        out_specs=row_spec,
        compiler_params=pltpu.CompilerParams(
            dimension_semantics=("arbitrary", "arbitrary"), vmem_limit_bytes=VMEM_LIMIT),
        name="merge_groups",
    )(x, *outs, *lses, w_out.astype(BF16))


def kernel(x, a_norm_g, a_w_in, a_b_if, a_hnorm_g, a_w_out, kv_norm_g, w_kv, b_norm_g, b_w_q,
           b_w_out, rel_bias, f_norm_g, f_w_up, f_conv_w, f_conv_b, f_w_down, final_norm_g):
    x = _mlstm_layer(x, a_norm_g[0], a_w_in[0], a_b_if[0], a_hnorm_g[0], a_w_out[0])
    x = _conv_ffn(x, f_norm_g[0], f_w_up[0], f_conv_w[0], f_conv_b[0], f_w_down[0],
                  final_norm_g, final_norm=False)
    att_w = len(B_GROUPS) * B_HEADS * B_HEAD_DIM
    outs, lses = [], []
    for g, (win, dil) in enumerate(B_GROUPS):
        cs = slice(g * D_MODEL, (g + 1) * D_MODEL)
        o, lse = _attn_group(x, g, win, dil, b_norm_g[0], kv_norm_g, b_w_q[0][:, cs],
                             w_kv[:, :att_w][:, cs], w_kv[:, att_w:][:, cs], rel_bias)
        outs.append(o)
        lses.append(lse)
    x = _merge_groups(x, outs, lses, b_w_out[0])
    return _conv_ffn(x, f_norm_g[1], f_w_up[1], f_conv_w[1], f_conv_b[1], f_w_down[1],
                     final_norm_g, final_norm=True)
```

```python
import functools
import math

import numpy as np
import jax
import jax.numpy as jnp
from jax import lax
from jax.experimental import pallas as pl
from jax.experimental.pallas import tpu as pltpu

F32 = jnp.float32
BF16 = jnp.bfloat16

D_MODEL = 1024
A_HEADS = 4
A_QK_DIM = 128
A_V_DIM = 256
A_NQ = A_HEADS * A_QK_DIM
A_NV = A_HEADS * A_V_DIM
A_Z_DIM = 2 * A_NQ + 2 * A_NV
GATE_SOFTCAP = 15.0
B_GROUPS = ((128, 1), (512, 4), (2048, 16))
B_HEAD_DIM = 64
B_HEADS = 16
B_BLOCK = 128
REL_BUCKETS = 32
REL_MAX_DIST = 2048
D_FF = 2816
EPS = 1e-6
MASK_VALUE = -1e30

LANES = 128
GATE_LANES = LANES
GATE_ROWS = 16
MLSTM_CHUNK = 256
FFN_ROWS = 512
FFN_COLS = 256
CONV_HALO = 8
ATT_ROWS = 512
MERGE_ROWS = 512
VMEM_LIMIT = 56 * 1024 * 1024


def _rms_scale(x):
    return lax.rsqrt(jnp.mean(x * x, axis=-1, keepdims=True) + EPS)


def _softcap(z):
    return GATE_SOFTCAP * jnp.tanh(z / GATE_SOFTCAP)


def _log_sigmoid(a):
    return jnp.minimum(a, 0.0) - jnp.log1p(jnp.exp(-jnp.abs(a)))


def _split3(v):
    hi = v.astype(BF16)
    r1 = v - hi.astype(F32)
    mid = r1.astype(BF16)
    lo = (r1 - mid.astype(F32)).astype(BF16)
    return hi, mid, lo


def _dot(a, b):
    return jnp.dot(a, b, preferred_element_type=F32)


def _dot_nt(a, b):
    return lax.dot_general(a, b, (((1,), (1,)), ((), ())), preferred_element_type=F32)


def _dot_tn(a, b):
    return lax.dot_general(a, b, (((0,), (0,)), ((), ())), preferred_element_type=F32)


def _const_spec(shape):
    return pl.BlockSpec(shape, lambda *_: (0,) * len(shape), pipeline_mode=pl.Buffered(1))


def _mlstm_kernel(x_ref, g_ref, wz_ref, wgc_ref, wgr_ref, bc_ref, br_ref, hg_ref, wo_ref,
                  out_ref, c_ref, n_ref, m_ref, hcat_ref):
    L = x_ref.shape[0]

    @pl.when(pl.program_id(1) == 0)
    def _():
        c_ref[...] = jnp.zeros_like(c_ref)
        n_ref[...] = jnp.zeros_like(n_ref)
        m_ref[...] = jnp.zeros_like(m_ref)

    x = x_ref[...]
    xn = (x * _rms_scale(x) * g_ref[...]).astype(BF16)
    z = _dot(xn, wz_ref[...])
    ac = _softcap(_dot(xn, wgc_ref[...]) + bc_ref[...])
    ar = _softcap(_dot_nt(wgr_ref[...], xn) + br_ref[...])
    lfc = _log_sigmoid(ac)
    lfr = _log_sigmoid(ar)

    row = lax.broadcasted_iota(jnp.int32, (L, L), 0)
    col = lax.broadcasted_iota(jnp.int32, (L, L), 1)
    causal = col <= row
    tril = jnp.where(causal, 1.0, 0.0).astype(BF16)
    triu = jnp.where(row <= col, 1.0, 0.0).astype(BF16)
    bcs = sum(_dot(tril, p) for p in _split3(lfc))
    brs = sum(_dot(p, triu) for p in _split3(lfr))

    for h in range(A_HEADS):
        q = z[:, h * A_QK_DIM:(h + 1) * A_QK_DIM] * (A_QK_DIM ** -0.5)
        k = z[:, A_NQ + h * A_QK_DIM:A_NQ + (h + 1) * A_QK_DIM]
        v = z[:, 2 * A_NQ + h * A_V_DIM:2 * A_NQ + (h + 1) * A_V_DIM]
        o = z[:, 2 * A_NQ + A_NV + h * A_V_DIM:2 * A_NQ + A_NV + (h + 1) * A_V_DIM]
        qb, kb, vb = q.astype(BF16), k.astype(BF16), v.astype(BF16)
        li_c = ac[:, h:h + 1]
        li_r = ar[h:h + 1, :]
        b_c = bcs[:, A_HEADS + h:A_HEADS + h + 1]
        b_r = brs[A_HEADS + h:A_HEADS + h + 1, :]
        m_prev = m_ref[h]
        c_prev = c_ref[h]
        n_prev = n_ref[h]

        log_d = jnp.where(causal, b_c - b_r + li_r, -jnp.inf)
        m_inter = b_c + m_prev
        m_t = jnp.maximum(m_inter, jnp.max(log_d, axis=-1, keepdims=True))
        sm = _dot_nt(qb, kb) * jnp.exp(log_d - m_t)
        w_inter = jnp.exp(m_inter - m_t)
        num = _dot(sm.astype(BF16), vb) + w_inter * _dot(qb, c_prev.astype(BF16))
        den = (jnp.sum(sm, axis=-1, keepdims=True)
               + w_inter * jnp.sum(q * n_prev, axis=-1, keepdims=True))
        hv = num * (1.0 / jnp.maximum(jnp.abs(den), jnp.exp(-m_t)))
        hv = hv * _rms_scale(hv) * hg_ref[:, h * A_V_DIM:(h + 1) * A_V_DIM]
        hcat_ref[:, h * A_V_DIM:(h + 1) * A_V_DIM] = (hv * jax.nn.sigmoid(o)).astype(BF16)

        b_last = b_c[L - 1:L, :]
        g_c = b_last - b_c + li_c
        g_r = b_last - b_r + li_r
        m_new = jnp.maximum(b_last + m_prev, jnp.max(g_r, axis=-1, keepdims=True))
        decay = jnp.exp(b_last + m_prev - m_new)
        kw = k * jnp.exp(g_c - m_new)
        c_ref[h] = decay * c_prev + _dot_tn(kw.astype(BF16), vb)
        n_ref[h] = decay * n_prev + jnp.sum(kw, axis=0, keepdims=True)
        m_ref[h] = m_new

    out_ref[...] = x + _dot(hcat_ref[...], wo_ref[...])


def _mlstm_layer(x, norm_g, w_in, b_if, hnorm_g, w_out):
    B, S, D = x.shape
    L = MLSTM_CHUNK
    wz = w_in[:, :A_Z_DIM].astype(BF16)
    wg = w_in[:, A_Z_DIM:]
    ng = 2 * A_HEADS
    wgc = jnp.pad(wg, ((0, 0), (0, GATE_LANES - ng))).astype(BF16)
    wgr = jnp.pad(wg.T, ((0, GATE_ROWS - ng), (0, 0))).astype(BF16)
    bc = jnp.pad(b_if[None, :], ((0, 0), (0, GATE_LANES - ng)))
    br = jnp.pad(b_if[:, None], ((0, GATE_ROWS - ng), (0, 0)))
    row_spec = pl.BlockSpec((None, L, D), lambda b, c: (b, c, 0))
    return pl.pallas_call(
        _mlstm_kernel,
        out_shape=jax.ShapeDtypeStruct((B, S, D), F32),
        grid=(B, S // L),
        in_specs=[row_spec, _const_spec((1, D)), _const_spec((D, A_Z_DIM)),
                  _const_spec((D, GATE_LANES)), _const_spec((GATE_ROWS, D)),
                  _const_spec((1, GATE_LANES)), _const_spec((GATE_ROWS, 1)),
                  _const_spec((1, A_NV)), _const_spec((A_NV, D))],
        out_specs=row_spec,
        scratch_shapes=[pltpu.VMEM((A_HEADS, A_QK_DIM, A_V_DIM), F32),
                        pltpu.VMEM((A_HEADS, 1, A_QK_DIM), F32),
                        pltpu.VMEM((A_HEADS, 1, 1), F32),
                        pltpu.VMEM((L, A_NV), BF16)],
        compiler_params=pltpu.CompilerParams(
            dimension_semantics=("arbitrary", "arbitrary"), vmem_limit_bytes=VMEM_LIMIT),
        name="mlstm_layer",
    )(x, norm_g[None, :], wz, wgc, wgr, bc, br, hnorm_g.reshape(1, A_NV), w_out.astype(BF16))


def _ffn_kernel(x_ref, g_ref, wup_ref, cw_ref, cb_ref, wdn_ref, fg_ref, *rest, final_norm):
    if final_norm:
        out_ref, ubuf_ref, act_ref = rest
    else:
        out_ref, xh0_ref, xh1_ref, xh2_ref, ubuf_ref, act_ref, slab_ref = rest
    T = x_ref.shape[0]
    H = CONV_HALO

    @pl.when(pl.program_id(1) == 0)
    def _():
        ubuf_ref[0:H, :] = jnp.zeros((H, 2 * D_FF), F32)

    x = x_ref[...]
    xn = (x * _rms_scale(x) * g_ref[...]).astype(BF16)
    cw = 2 * FFN_COLS
    for j in range(2 * D_FF // cw):
        ubuf_ref[H:H + T, j * cw:(j + 1) * cw] = _dot(xn, wup_ref[:, j * cw:(j + 1) * cw])

    def conv(c0):
        cs = slice(c0, c0 + FFN_COLS)
        return (ubuf_ref[H - 2:H - 2 + T, cs] * cw_ref[0:1, cs]
                + ubuf_ref[H - 1:H - 1 + T, cs] * cw_ref[1:2, cs]
                + ubuf_ref[H:H + T, cs] * cw_ref[2:3, cs] + cb_ref[:, cs])

    for j in range(D_FF // FFN_COLS):
        gate = conv(j * FFN_COLS)
        val = conv(D_FF + j * FFN_COLS)
        act_ref[:, j * FFN_COLS:(j + 1) * FFN_COLS] = (gate * jax.nn.sigmoid(gate) * val).astype(BF16)

    ubuf_ref[0:H, :] = ubuf_ref[T:T + H, :]
    y = x + _dot(act_ref[...], wdn_ref[...])
    if final_norm:
        out_ref[...] = y * _rms_scale(y) * fg_ref[...]
        return
    out_ref[...] = y
    xh = y * _rms_scale(y)
    xh0_ref[...] = xh.astype(BF16)
    nslab = D_MODEL // LANES
    for k in range(nslab):
        slab_ref[k] = xh[:, k * LANES:(k + 1) * LANES]
    for dil, ref in ((B_GROUPS[1][1], xh1_ref), (B_GROUPS[2][1], xh2_ref)):
        for r in range(dil):
            for k in range(nslab):
                c0 = r * D_MODEL + k * LANES
                ref[:, c0:c0 + LANES] = slab_ref[k, pl.ds(r, T // dil, stride=dil), :].astype(BF16)


def _conv_ffn(x, norm_g, w_up, conv_w, conv_b, w_down, final_g, final_norm):
    B, S, D = x.shape
    T = FFN_ROWS
    row_spec = pl.BlockSpec((None, T, D), lambda b, t: (b, t, 0))
    out_shape = [jax.ShapeDtypeStruct((B, S, D), F32)]
    out_specs = [row_spec]
    scratch = [pltpu.VMEM((T + CONV_HALO, 2 * D_FF), F32), pltpu.VMEM((T, D_FF), BF16)]
    if not final_norm:
        out_shape.append(jax.ShapeDtypeStruct((B, S, D), BF16))
        out_specs.append(row_spec)
        for _, dil in B_GROUPS[1:]:
            out_shape.append(jax.ShapeDtypeStruct((B, S // dil, dil * D), BF16))
            out_specs.append(pl.BlockSpec((None, T // dil, dil * D), lambda b, t: (b, t, 0)))
        scratch.append(pltpu.VMEM((D // LANES, T, LANES), F32))
    res = pl.pallas_call(
        functools.partial(_ffn_kernel, final_norm=final_norm),
        out_shape=out_shape,
        grid=(B, S // T),
        in_specs=[row_spec, _const_spec((1, D)), _const_spec((D, 2 * D_FF)),
                  _const_spec((3, 2 * D_FF)), _const_spec((1, 2 * D_FF)),
                  _const_spec((D_FF, D)), _const_spec((1, D))],
        out_specs=out_specs,
        scratch_shapes=scratch,
        compiler_params=pltpu.CompilerParams(
            dimension_semantics=("arbitrary", "arbitrary"), vmem_limit_bytes=VMEM_LIMIT),
        name="conv_ffn_final" if final_norm else "conv_ffn",
    )(x, norm_g[None, :], w_up.astype(BF16), conv_w, conv_b[None, :], w_down.astype(BF16),
      final_g[None, :])
    return res[0] if final_norm else res


def _attn_kernel(xh_ref, wq_ref, wk_ref, wv_ref, bias_ref, o_ref, lse_ref,
                 qe_ref, qo_ref, k_ref, v_ref):
    T = xh_ref.shape[0]
    P = B_BLOCK
    i = pl.program_id(2)

    @pl.when(i == 0)
    def _():
        k_ref[0:P, :] = jnp.zeros((P, D_MODEL), BF16)
        v_ref[0:P, :] = jnp.zeros((P, D_MODEL), BF16)

    @pl.when(i > 0)
    def _():
        k_ref[0:P, :] = k_ref[T:T + P, :]
        v_ref[0:P, :] = v_ref[T:T + P, :]

    xh = xh_ref[...]
    lane = lax.broadcasted_iota(jnp.int32, (T, D_MODEL), 1)
    even_head = (lane % (2 * B_HEAD_DIM)) < B_HEAD_DIM
    q = _dot(xh, wq_ref[...]) * (B_HEAD_DIM ** -0.5)
    qe_ref[...] = jnp.where(even_head, q, 0.0).astype(BF16)
    qo_ref[...] = jnp.where(even_head, 0.0, q).astype(BF16)
    k_ref[P:P + T, :] = _dot(xh, wk_ref[...]).astype(BF16)
    v_ref[P:P + T, :] = _dot(xh, wv_ref[...]).astype(BF16)

    lane2 = lax.broadcasted_iota(jnp.int32, (P, 2 * B_HEAD_DIM), 1)
    first_half = lane2 < B_HEAD_DIM
    kcol = lax.broadcasted_iota(jnp.int32, (1, 2 * P), 1)

    def qblock(j, carry):
        r0 = pl.multiple_of(j * P, P)
        no_prev = jnp.where((kcol < P) & (i == 0) & (j == 0), MASK_VALUE, 0.0)
        lse_all = jnp.zeros((P, LANES), F32)
        for hp in range(B_HEADS // 2):
            cs = slice(hp * 2 * B_HEAD_DIM, (hp + 1) * 2 * B_HEAD_DIM)
            k2 = k_ref[pl.ds(r0, 2 * P), cs]
            v2 = v_ref[pl.ds(r0, 2 * P), cs]
            outs = []
            for half, q_ref in enumerate((qe_ref, qo_ref)):
                s = _dot_nt(q_ref[pl.ds(r0, P), cs], k2) + bias_ref[2 * hp + half] + no_prev
                m = jnp.max(s, axis=-1, keepdims=True)
                p = jnp.exp(s - m)
                l = jnp.sum(p, axis=-1, keepdims=True)
                outs.append(_dot(p.astype(BF16), v2) * (1.0 / l))
                lse_all = jnp.where(lane2 == 2 * hp + half, m + jnp.log(l), lse_all)
            o_ref[pl.ds(r0, P), cs] = jnp.where(first_half, outs[0], outs[1]).astype(BF16)
        lse_ref[pl.ds(r0, P), :] = lse_all
        return carry

    lax.fori_loop(0, T // P, qblock, 0)


def _t5_bucket(dist):
    max_exact = REL_BUCKETS // 2
    d = np.maximum(dist, 0)
    log_ratio = np.log(np.maximum(d, 1) / max_exact) / math.log(REL_MAX_DIST / max_exact)
    large = np.minimum(max_exact + (log_ratio * (REL_BUCKETS - max_exact)).astype(np.int64),
                       REL_BUCKETS - 1)
    return np.where(d < max_exact, d, large).astype(np.int32)


def _band_bias(rel_bias, g, win, dil):
    P = B_BLOCK
    n = 3 * P - 1
    delta = np.arange(n) - (P - 1)
    valid = (delta >= 0) & (delta <= win // dil)
    vec = rel_bias[_t5_bucket(delta * dil)][:, g * B_HEADS:(g + 1) * B_HEADS].astype(F32)
    vec = jnp.where(valid[:, None], vec, MASK_VALUE).T
    hank = jnp.tile(vec, (1, P + 1))[:, :P * (n + 1)].reshape(B_HEADS, P, n + 1)[:, :, :2 * P]
    return hank[:, :, ::-1]


def _attn_group(xh, g, win, dil, wq, wk, wv, rel_bias):
    B, n, _ = xh.shape
    D = D_MODEL
    T = ATT_ROWS
    cls_spec = pl.BlockSpec((None, T, D), lambda b, r, i: (b, i, r))
    lse_spec = pl.BlockSpec((None, T, LANES), lambda b, r, i: (b, i, r))
    return pl.pallas_call(
        _attn_kernel,
        out_shape=[jax.ShapeDtypeStruct((B, n, dil * D), BF16),
                   jax.ShapeDtypeStruct((B, n, dil * LANES), F32)],
        grid=(B, dil, n // T),
        in_specs=[cls_spec, _const_spec((D, D)), _const_spec((D, D)), _const_spec((D, D)),
                  _const_spec((B_HEADS, B_BLOCK, 2 * B_BLOCK))],
        out_specs=[cls_spec, lse_spec],
        scratch_shapes=[pltpu.VMEM((T, D), BF16), pltpu.VMEM((T, D), BF16),
                        pltpu.VMEM((T + B_BLOCK, D), BF16), pltpu.VMEM((T + B_BLOCK, D), BF16)],
        compiler_params=pltpu.CompilerParams(
            dimension_semantics=("arbitrary", "arbitrary", "arbitrary"),
            vmem_limit_bytes=VMEM_LIMIT),
        name=f"dilated_attn_g{g}",
    )(xh, wq, wk, wv, _band_bias(rel_bias, g, win, dil))


def _merge_kernel(x_ref, o0_ref, o1_ref, o2_ref, l0_ref, l1_ref, l2_ref, ex_ref, wo_ref, out_ref,
                  os1_ref, os2_ref, ls1_ref, ls2_ref, mg_ref):
    T = x_ref.shape[0]
    nslab = D_MODEL // LANES
    for dil, o_src, o_dst, l_src, l_dst in ((B_GROUPS[1][1], o1_ref, os1_ref, l1_ref, ls1_ref),
                                            (B_GROUPS[2][1], o2_ref, os2_ref, l2_ref, ls2_ref)):
        for r in range(dil):
            rows = pl.ds(r, T // dil, stride=dil)
            l_dst[rows, :] = l_src[:, r * LANES:(r + 1) * LANES]
            for k in range(nslab):
                c0 = r * D_MODEL + k * LANES
                o_dst[k, rows, :] = o_src[:, c0:c0 + LANES].astype(F32)
    l0, l1, l2 = l0_ref[...], ls1_ref[...], ls2_ref[...]
    m = jnp.maximum(jnp.maximum(l0, l1), l2)
    e0, e1, e2 = jnp.exp(l0 - m), jnp.exp(l1 - m), jnp.exp(l2 - m)
    inv = 1.0 / (e0 + e1 + e2)

    def spread(w):
        hi = w.astype(BF16)
        lo = (w - hi.astype(F32)).astype(BF16)
        return _dot(hi, ex_ref[...]) + _dot(lo, ex_ref[...])

    w0, w1, w2 = spread(e0 * inv), spread(e1 * inv), spread(e2 * inv)
    for k in range(nslab):
        ks = slice(k * LANES, (k + 1) * LANES)
        mg = w0[:, ks] * o0_ref[:, ks].astype(F32) + w1[:, ks] * os1_ref[k] + w2[:, ks] * os2_ref[k]
        mg_ref[:, ks] = mg.astype(BF16)
    out_ref[...] = x_ref[...] + _dot(mg_ref[...], wo_ref[...])


def _merge_groups(x, outs, lses, w_out):
    B, S, D = x.shape
    T = MERGE_ROWS
    row_spec = pl.BlockSpec((None, T, D), lambda b, t: (b, t, 0))
    o_specs = [pl.BlockSpec((None, T // dil, dil * D), lambda b, t: (b, t, 0)) for _, dil in B_GROUPS]
    l_specs = [pl.BlockSpec((None, T // dil, dil * LANES), lambda b, t: (b, t, 0))
               for _, dil in B_GROUPS]
    expand = np.zeros((LANES, D), np.float32)
    for h in range(B_HEADS):
        expand[h, h * B_HEAD_DIM:(h + 1) * B_HEAD_DIM] = 1.0
    nslab = D // LANES
    return pl.pallas_call(
        _merge_kernel,
        out_shape=jax.ShapeDtypeStruct((B, S, D), F32),
        grid=(B, S // T),
        in_specs=[row_spec] + o_specs + l_specs + [_const_spec((LANES, D)), _const_spec((D, D))],
        out_specs=row_spec,
        scratch_shapes=[pltpu.VMEM((nslab, T, LANES), F32), pltpu.VMEM((nslab, T, LANES), F32),
                        pltpu.VMEM((T, LANES), F32), pltpu.VMEM((T, LANES), F32),
                        pltpu.VMEM((T, D), BF16)],
        compiler_params=pltpu.CompilerParams(
            dimension_semantics=("arbitrary", "arbitrary"), vmem_limit_bytes=VMEM_LIMIT),
        name="merge_groups",
    )(x, *outs, *lses, jnp.asarray(expand, BF16), w_out.astype(BF16))


def kernel(x, a_norm_g, a_w_in, a_b_if, a_hnorm_g, a_w_out, kv_norm_g, w_kv, b_norm_g, b_w_q,
           b_w_out, rel_bias, f_norm_g, f_w_up, f_conv_w, f_conv_b, f_w_down, final_norm_g):
    x = _mlstm_layer(x, a_norm_g[0], a_w_in[0], a_b_if[0], a_hnorm_g[0], a_w_out[0])
    x, *streams = _conv_ffn(x, f_norm_g[0], f_w_up[0], f_conv_w[0], f_conv_b[0], f_w_down[0],
                            final_norm_g, final_norm=False)
    att_w = len(B_GROUPS) * B_HEADS * B_HEAD_DIM
    wq_all = (b_norm_g[0][:, None] * b_w_q[0]).astype(BF16)
    wkv_all = (kv_norm_g[:, None] * w_kv).astype(BF16)
    outs, lses = [], []
    for g, (win, dil) in enumerate(B_GROUPS):
        cs = slice(g * D_MODEL, (g + 1) * D_MODEL)
        o, lse = _attn_group(streams[g], g, win, dil, wq_all[:, cs], wkv_all[:, :att_w][:, cs],
                             wkv_all[:, att_w:][:, cs], rel_bias)
        outs.append(o)
        lses.append(lse)
    x = _merge_groups(x, outs, lses, b_w_out[0])
    return _conv_ffn(x, f_norm_g[1], f_w_up[1], f_conv_w[1], f_conv_b[1], f_w_down[1],
                     final_norm_g, final_norm=True)
```

```python
import functools
import math

import numpy as np
import jax
import jax.numpy as jnp
from jax import lax
from jax.experimental import pallas as pl
from jax.experimental.pallas import tpu as pltpu

F32 = jnp.float32
BF16 = jnp.bfloat16

D_MODEL = 1024
A_HEADS = 4
A_QK_DIM = 128
A_V_DIM = 256
A_NQ = A_HEADS * A_QK_DIM
A_NV = A_HEADS * A_V_DIM
A_Z_DIM = 2 * A_NQ + 2 * A_NV
GATE_SOFTCAP = 15.0
B_GROUPS = ((128, 1), (512, 4), (2048, 16))
B_HEAD_DIM = 64
B_HEADS = 16
B_BLOCK = 128
REL_BUCKETS = 32
REL_MAX_DIST = 2048
D_FF = 2816
EPS = 1e-6
MASK_VALUE = -1e30

LANES = 128
GATE_LANES = LANES
GATE_ROWS = 16
MLSTM_CHUNK = 256
FFN_ROWS = 512
FFN_COLS = 256
CONV_HALO = 8
ATT_ROWS = 512
MERGE_ROWS = 512
VMEM_LIMIT = 56 * 1024 * 1024


def _rms_scale(x):
    return lax.rsqrt(jnp.mean(x * x, axis=-1, keepdims=True) + EPS)


def _softcap(z):
    return GATE_SOFTCAP * jnp.tanh(z / GATE_SOFTCAP)


def _log_sigmoid(a):
    return jnp.minimum(a, 0.0) - jnp.log1p(jnp.exp(-jnp.abs(a)))


def _split3(v):
    hi = v.astype(BF16)
    r1 = v - hi.astype(F32)
    mid = r1.astype(BF16)
    lo = (r1 - mid.astype(F32)).astype(BF16)
    return hi, mid, lo


def _dot(a, b):
    return jnp.dot(a, b, preferred_element_type=F32)


def _dot_nt(a, b):
    return lax.dot_general(a, b, (((1,), (1,)), ((), ())), preferred_element_type=F32)


def _dot_tn(a, b):
    return lax.dot_general(a, b, (((0,), (0,)), ((), ())), preferred_element_type=F32)


def _const_spec(shape):
    return pl.BlockSpec(shape, lambda *_: (0,) * len(shape), pipeline_mode=pl.Buffered(1))


def _mlstm_kernel(x_ref, g_ref, wz_ref, wgc_ref, wgr_ref, bc_ref, br_ref, hg_ref, wo_ref,
                  out_ref, c_ref, n_ref, m_ref, hcat_ref):
    NB, L = x_ref.shape[0], x_ref.shape[1]

    @pl.when(pl.program_id(0) == 0)
    def _():
        c_ref[...] = jnp.zeros_like(c_ref)
        n_ref[...] = jnp.zeros_like(n_ref)
        m_ref[...] = jnp.zeros_like(m_ref)

    row = lax.broadcasted_iota(jnp.int32, (L, L), 0)
    col = lax.broadcasted_iota(jnp.int32, (L, L), 1)
    causal = col <= row
    tril = jnp.where(causal, 1.0, 0.0).astype(BF16)
    triu = jnp.where(row <= col, 1.0, 0.0).astype(BF16)

    proj = []
    for b in range(NB):
        x = x_ref[b]
        xn = (x * _rms_scale(x) * g_ref[...]).astype(BF16)
        z = _dot(xn, wz_ref[...])
        ac = _softcap(_dot(xn, wgc_ref[...]) + bc_ref[...])
        ar = _softcap(_dot_nt(wgr_ref[...], xn) + br_ref[...])
        bcs = sum(_dot(tril, p) for p in _split3(_log_sigmoid(ac)))
        brs = sum(_dot(p, triu) for p in _split3(_log_sigmoid(ar)))
        proj.append((z, ac, ar, bcs, brs))

    for h, b in [(h, b) for h in range(A_HEADS) for b in range(NB)]:
        z, ac, ar, bcs, brs = proj[b]
        st = b * A_HEADS + h
        q = z[:, h * A_QK_DIM:(h + 1) * A_QK_DIM] * (A_QK_DIM ** -0.5)
        k = z[:, A_NQ + h * A_QK_DIM:A_NQ + (h + 1) * A_QK_DIM]
        v = z[:, 2 * A_NQ + h * A_V_DIM:2 * A_NQ + (h + 1) * A_V_DIM]
        o = z[:, 2 * A_NQ + A_NV + h * A_V_DIM:2 * A_NQ + A_NV + (h + 1) * A_V_DIM]
        qb, kb, vb = q.astype(BF16), k.astype(BF16), v.astype(BF16)
        li_c = ac[:, h:h + 1]
        li_r = ar[h:h + 1, :]
        b_c = bcs[:, A_HEADS + h:A_HEADS + h + 1]
        b_r = brs[A_HEADS + h:A_HEADS + h + 1, :]
        m_prev = m_ref[st]
        c_prev = c_ref[st]
        n_prev = n_ref[st]

        log_d = jnp.where(causal, b_c - b_r + li_r, -jnp.inf)
        m_inter = b_c + m_prev
        m_t = jnp.maximum(m_inter, jnp.max(log_d, axis=-1, keepdims=True))
        sm = _dot_nt(qb, kb) * jnp.exp(log_d - m_t)
        w_inter = jnp.exp(m_inter - m_t)
        num = _dot(sm.astype(BF16), vb) + w_inter * _dot(qb, c_prev.astype(BF16))
        den = (jnp.sum(sm, axis=-1, keepdims=True)
               + w_inter * jnp.sum(q * n_prev, axis=-1, keepdims=True))
        hv = num * (1.0 / jnp.maximum(jnp.abs(den), jnp.exp(-m_t)))
        hv = hv * _rms_scale(hv) * hg_ref[:, h * A_V_DIM:(h + 1) * A_V_DIM]
        hcat_ref[b, :, h * A_V_DIM:(h + 1) * A_V_DIM] = (hv * jax.nn.sigmoid(o)).astype(BF16)

        b_last = b_c[L - 1:L, :]
        g_c = b_last - b_c + li_c
        g_r = b_last - b_r + li_r
        m_new = jnp.maximum(b_last + m_prev, jnp.max(g_r, axis=-1, keepdims=True))
        decay = jnp.exp(b_last + m_prev - m_new)
        kw = k * jnp.exp(g_c - m_new)
        c_ref[st] = decay * c_prev + _dot_tn(kw.astype(BF16), vb)
        n_ref[st] = decay * n_prev + jnp.sum(kw, axis=0, keepdims=True)
        m_ref[st] = m_new

    for b in range(NB):
        out_ref[b] = x_ref[b] + _dot(hcat_ref[b], wo_ref[...])


def _mlstm_layer(x, norm_g, w_in, b_if, hnorm_g, w_out):
    B, S, D = x.shape
    L = MLSTM_CHUNK
    wz = w_in[:, :A_Z_DIM].astype(BF16)
    wg = w_in[:, A_Z_DIM:]
    ng = 2 * A_HEADS
    wgc = jnp.pad(wg, ((0, 0), (0, GATE_LANES - ng))).astype(BF16)
    wgr = jnp.pad(wg.T, ((0, GATE_ROWS - ng), (0, 0))).astype(BF16)
    bc = jnp.pad(b_if[None, :], ((0, 0), (0, GATE_LANES - ng)))
    br = jnp.pad(b_if[:, None], ((0, GATE_ROWS - ng), (0, 0)))
    row_spec = pl.BlockSpec((B, L, D), lambda c: (0, c, 0))
    return pl.pallas_call(
        _mlstm_kernel,
        out_shape=jax.ShapeDtypeStruct((B, S, D), F32),
        grid=(S // L,),
        in_specs=[row_spec, _const_spec((1, D)), _const_spec((D, A_Z_DIM)),
                  _const_spec((D, GATE_LANES)), _const_spec((GATE_ROWS, D)),
                  _const_spec((1, GATE_LANES)), _const_spec((GATE_ROWS, 1)),
                  _const_spec((1, A_NV)), _const_spec((A_NV, D))],
        out_specs=row_spec,
        scratch_shapes=[pltpu.VMEM((B * A_HEADS, A_QK_DIM, A_V_DIM), F32),
                        pltpu.VMEM((B * A_HEADS, 1, A_QK_DIM), F32),
                        pltpu.VMEM((B * A_HEADS, 1, 1), F32),
                        pltpu.VMEM((B, L, A_NV), BF16)],
        compiler_params=pltpu.CompilerParams(
            dimension_semantics=("arbitrary",), vmem_limit_bytes=VMEM_LIMIT),
        name="mlstm_layer",
    )(x, norm_g[None, :], wz, wgc, wgr, bc, br, hnorm_g.reshape(1, A_NV), w_out.astype(BF16))


def _ffn_kernel(x_ref, g_ref, wup_ref, cw_ref, cb_ref, wdn_ref, fg_ref, *rest, final_norm):
    if final_norm:
        out_ref, ubuf_ref, act_ref = rest
    else:
        out_ref, xh0_ref, xh1_ref, xh2_ref, ubuf_ref, act_ref, slab_ref = rest
    T = x_ref.shape[0]
    H = CONV_HALO

    @pl.when(pl.program_id(1) == 0)
    def _():
        ubuf_ref[0:H, :] = jnp.zeros((H, 2 * D_FF), F32)

    x = x_ref[...]
    xn = (x * _rms_scale(x) * g_ref[...]).astype(BF16)
    cw = 2 * FFN_COLS
    for j in range(2 * D_FF // cw):
        ubuf_ref[H:H + T, j * cw:(j + 1) * cw] = _dot(xn, wup_ref[:, j * cw:(j + 1) * cw])

    def conv(c0):
        cs = slice(c0, c0 + FFN_COLS)
        return (ubuf_ref[H - 2:H - 2 + T, cs] * cw_ref[0:1, cs]
                + ubuf_ref[H - 1:H - 1 + T, cs] * cw_ref[1:2, cs]
                + ubuf_ref[H:H + T, cs] * cw_ref[2:3, cs] + cb_ref[:, cs])

    for j in range(D_FF // FFN_COLS):
        gate = conv(j * FFN_COLS)
        val = conv(D_FF + j * FFN_COLS)
        act_ref[:, j * FFN_COLS:(j + 1) * FFN_COLS] = (gate * jax.nn.sigmoid(gate) * val).astype(BF16)

    ubuf_ref[0:H, :] = ubuf_ref[T:T + H, :]
    y = x + _dot(act_ref[...], wdn_ref[...])
    if final_norm:
        out_ref[...] = y * _rms_scale(y) * fg_ref[...]
        return
    out_ref[...] = y
    xh = y * _rms_scale(y)
    xh0_ref[...] = xh.astype(BF16)
    nslab = D_MODEL // LANES
    for k in range(nslab):
        slab_ref[k] = xh[:, k * LANES:(k + 1) * LANES]
    for dil, ref in ((B_GROUPS[1][1], xh1_ref), (B_GROUPS[2][1], xh2_ref)):
        for r in range(dil):
            for k in range(nslab):
                c0 = r * D_MODEL + k * LANES
                ref[:, c0:c0 + LANES] = slab_ref[k, pl.ds(r, T // dil, stride=dil), :].astype(BF16)


def _conv_ffn(x, norm_g, w_up, conv_w, conv_b, w_down, final_g, final_norm):
    B, S, D = x.shape
    T = FFN_ROWS
    row_spec = pl.BlockSpec((None, T, D), lambda b, t: (b, t, 0))
    out_shape = [jax.ShapeDtypeStruct((B, S, D), F32)]
    out_specs = [row_spec]
    scratch = [pltpu.VMEM((T + CONV_HALO, 2 * D_FF), F32), pltpu.VMEM((T, D_FF), BF16)]
    if not final_norm:
        out_shape.append(jax.ShapeDtypeStruct((B, S, D), BF16))
        out_specs.append(row_spec)
        for _, dil in B_GROUPS[1:]:
            out_shape.append(jax.ShapeDtypeStruct((B, S // dil, dil * D), BF16))
            out_specs.append(pl.BlockSpec((None, T // dil, dil * D), lambda b, t: (b, t, 0)))
        scratch.append(pltpu.VMEM((D // LANES, T, LANES), F32))
    res = pl.pallas_call(
        functools.partial(_ffn_kernel, final_norm=final_norm),
        out_shape=out_shape,
        grid=(B, S // T),
        in_specs=[row_spec, _const_spec((1, D)), _const_spec((D, 2 * D_FF)),
                  _const_spec((3, 2 * D_FF)), _const_spec((1, 2 * D_FF)),
                  _const_spec((D_FF, D)), _const_spec((1, D))],
        out_specs=out_specs,
        scratch_shapes=scratch,
        compiler_params=pltpu.CompilerParams(
            dimension_semantics=("arbitrary", "arbitrary"), vmem_limit_bytes=VMEM_LIMIT),
        name="conv_ffn_final" if final_norm else "conv_ffn",
    )(x, norm_g[None, :], w_up.astype(BF16), conv_w, conv_b[None, :], w_down.astype(BF16),
      final_g[None, :])
    return res[0] if final_norm else res


def _attn_kernel(xh_ref, wq_ref, wkt_ref, wv_ref, bias_ref, o_ref, lse_ref,
                 q2_ref, kt_ref, v_ref):
    T = xh_ref.shape[0]
    P = B_BLOCK
    i = pl.program_id(2)

    @pl.when(i == 0)
    def _():
        kt_ref[:, 0:P] = jnp.zeros((D_MODEL, P), BF16)
        v_ref[0:P, :] = jnp.zeros((P, D_MODEL), BF16)

    @pl.when(i > 0)
    def _():
        kt_ref[:, 0:P] = kt_ref[:, T:T + P]
        v_ref[0:P, :] = v_ref[T:T + P, :]

    xh = xh_ref[...]
    lane = lax.broadcasted_iota(jnp.int32, (P, D_MODEL), 1)
    even_head = (lane % (2 * B_HEAD_DIM)) < B_HEAD_DIM
    q = _dot(xh, wq_ref[...]) * (B_HEAD_DIM ** -0.5)
    for j in range(T // P):
        qj = q[j * P:(j + 1) * P]
        q2_ref[j, 0:P, :] = jnp.where(even_head, qj, 0.0).astype(BF16)
        q2_ref[j, P:2 * P, :] = jnp.where(even_head, 0.0, qj).astype(BF16)
    kt_ref[:, P:P + T] = _dot_nt(wkt_ref[...], xh).astype(BF16)
    v_ref[P:P + T, :] = _dot(xh, wv_ref[...]).astype(BF16)

    lane2 = lax.broadcasted_iota(jnp.int32, (P, 2 * B_HEAD_DIM), 1)
    first_half = lane2 < B_HEAD_DIM
    first = (i == 0).astype(jnp.int32)

    for j in range(T // P):
        lse_all = jnp.zeros((P, LANES), F32)
        for hp in range(B_HEADS // 2):
            cs = slice(hp * 2 * B_HEAD_DIM, (hp + 1) * 2 * B_HEAD_DIM)
            s = (_dot(q2_ref[j, :, cs], kt_ref[cs, j * P:(j + 2) * P])
                 + bias_ref[first if j == 0 else 0, hp])
            m = jnp.max(s, axis=-1, keepdims=True)
            p = jnp.exp(s - m)
            l = jnp.sum(p, axis=-1, keepdims=True)
            pv = _dot(p.astype(BF16), v_ref[j * P:(j + 2) * P, cs]) * (1.0 / l)
            lse = m + jnp.log(l)
            o_ref[j * P:(j + 1) * P, cs] = jnp.where(first_half, pv[0:P], pv[P:2 * P]).astype(BF16)
            lse_all = jnp.where(lane2 == 2 * hp, lse[0:P],
                                jnp.where(lane2 == 2 * hp + 1, lse[P:2 * P], lse_all))
        lse_ref[j * P:(j + 1) * P, :] = lse_all


def _t5_bucket(dist):
    max_exact = REL_BUCKETS // 2
    d = np.maximum(dist, 0)
    log_ratio = np.log(np.maximum(d, 1) / max_exact) / math.log(REL_MAX_DIST / max_exact)
    large = np.minimum(max_exact + (log_ratio * (REL_BUCKETS - max_exact)).astype(np.int64),
                       REL_BUCKETS - 1)
    return np.where(d < max_exact, d, large).astype(np.int32)


def _band_bias(rel_bias, g, win, dil):
    P = B_BLOCK
    n = 3 * P - 1
    delta = np.arange(n) - (P - 1)
    valid = (delta >= 0) & (delta <= win // dil)
    vec = rel_bias[_t5_bucket(delta * dil)][:, g * B_HEADS:(g + 1) * B_HEADS].astype(F32)
    vec = jnp.where(valid[:, None], vec, MASK_VALUE).T
    hank = jnp.tile(vec, (1, P + 1))[:, :P * (n + 1)].reshape(B_HEADS, P, n + 1)[:, :, :2 * P]
    bias = hank[:, :, ::-1]
    bias = bias.reshape(B_HEADS // 2, 2 * P, 2 * P)
    no_prev = jnp.where(np.arange(2 * P) < P, MASK_VALUE, bias)
    return jnp.stack([bias, no_prev])


def _attn_group(xh, g, win, dil, wq, wk, wv, rel_bias):
    B, n, _ = xh.shape
    D = D_MODEL
    T = ATT_ROWS
    cls_spec = pl.BlockSpec((None, T, D), lambda b, r, i: (b, i, r))
    lse_spec = pl.BlockSpec((None, T, LANES), lambda b, r, i: (b, i, r))
    return pl.pallas_call(
        _attn_kernel,
        out_shape=[jax.ShapeDtypeStruct((B, n, dil * D), BF16),
                   jax.ShapeDtypeStruct((B, n, dil * LANES), F32)],
        grid=(B, dil, n // T),
        in_specs=[cls_spec, _const_spec((D, D)), _const_spec((D, D)), _const_spec((D, D)),
                  _const_spec((2, B_HEADS // 2, 2 * B_BLOCK, 2 * B_BLOCK))],
        out_specs=[cls_spec, lse_spec],
        scratch_shapes=[pltpu.VMEM((T // B_BLOCK, 2 * B_BLOCK, D), BF16),
                        pltpu.VMEM((D, T + B_BLOCK), BF16), pltpu.VMEM((T + B_BLOCK, D), BF16)],
        compiler_params=pltpu.CompilerParams(
            dimension_semantics=("arbitrary", "arbitrary", "arbitrary"),
            vmem_limit_bytes=VMEM_LIMIT),
        name=f"dilated_attn_g{g}",
    )(xh, wq, wk.T, wv, _band_bias(rel_bias, g, win, dil))


def _merge_kernel(x_ref, o0_ref, o1_ref, o2_ref, l0_ref, l1_ref, l2_ref, ex_ref, wo_ref, out_ref,
                  os1_ref, os2_ref, ls1_ref, ls2_ref, mg_ref):
    T = x_ref.shape[0]
    nslab = D_MODEL // LANES
    for dil, o_src, o_dst, l_src, l_dst in ((B_GROUPS[1][1], o1_ref, os1_ref, l1_ref, ls1_ref),
                                            (B_GROUPS[2][1], o2_ref, os2_ref, l2_ref, ls2_ref)):
        for r in range(dil):
            rows = pl.ds(r, T // dil, stride=dil)
            l_dst[rows, :] = l_src[:, r * LANES:(r + 1) * LANES]
            for k in range(nslab):
                c0 = r * D_MODEL + k * LANES
                o_dst[k, rows, :] = o_src[:, c0:c0 + LANES].astype(F32)
    l0, l1, l2 = l0_ref[...], ls1_ref[...], ls2_ref[...]
    m = jnp.maximum(jnp.maximum(l0, l1), l2)
    e0, e1, e2 = jnp.exp(l0 - m), jnp.exp(l1 - m), jnp.exp(l2 - m)
    inv = 1.0 / (e0 + e1 + e2)

    def spread(w):
        return _dot(w.astype(BF16), ex_ref[...])

    w0, w1, w2 = spread(e0 * inv), spread(e1 * inv), spread(e2 * inv)
    for k in range(nslab):
        ks = slice(k * LANES, (k + 1) * LANES)
        mg = w0[:, ks] * o0_ref[:, ks].astype(F32) + w1[:, ks] * os1_ref[k] + w2[:, ks] * os2_ref[k]
        mg_ref[:, ks] = mg.astype(BF16)
    out_ref[...] = x_ref[...] + _dot(mg_ref[...], wo_ref[...])


def _merge_groups(x, outs, lses, w_out):
    B, S, D = x.shape
    T = MERGE_ROWS
    row_spec = pl.BlockSpec((None, T, D), lambda b, t: (b, t, 0))
    o_specs = [pl.BlockSpec((None, T // dil, dil * D), lambda b, t: (b, t, 0)) for _, dil in B_GROUPS]
    l_specs = [pl.BlockSpec((None, T // dil, dil * LANES), lambda b, t: (b, t, 0))
               for _, dil in B_GROUPS]
    expand = np.zeros((LANES, D), np.float32)
    for h in range(B_HEADS):
        expand[h, h * B_HEAD_DIM:(h + 1) * B_HEAD_DIM] = 1.0
    nslab = D // LANES
    return pl.pallas_call(
        _merge_kernel,
        out_shape=jax.ShapeDtypeStruct((B, S, D), F32),
        grid=(B, S // T),
        in_specs=[row_spec] + o_specs + l_specs + [_const_spec((LANES, D)), _const_spec((D, D))],
        out_specs=row_spec,
        scratch_shapes=[pltpu.VMEM((nslab, T, LANES), F32), pltpu.VMEM((nslab, T, LANES), F32),
                        pltpu.VMEM((T, LANES), F32), pltpu.VMEM((T, LANES), F32),
                        pltpu.VMEM((T, D), BF16)],
        compiler_params=pltpu.CompilerParams(
            dimension_semantics=("arbitrary", "arbitrary"), vmem_limit_bytes=VMEM_LIMIT),
        name="merge_groups",
    )(x, *outs, *lses, jnp.asarray(expand, BF16), w_out.astype(BF16))


def kernel(x, a_norm_g, a_w_in, a_b_if, a_hnorm_g, a_w_out, kv_norm_g, w_kv, b_norm_g, b_w_q,
           b_w_out, rel_bias, f_norm_g, f_w_up, f_conv_w, f_conv_b, f_w_down, final_norm_g):
    x = _mlstm_layer(x, a_norm_g[0], a_w_in[0], a_b_if[0], a_hnorm_g[0], a_w_out[0])
    x, *streams = _conv_ffn(x, f_norm_g[0], f_w_up[0], f_conv_w[0], f_conv_b[0], f_w_down[0],
                            final_norm_g, final_norm=False)
    att_w = len(B_GROUPS) * B_HEADS * B_HEAD_DIM
    wq_all = (b_norm_g[0][:, None] * b_w_q[0]).astype(BF16)
    wkv_all = (kv_norm_g[:, None] * w_kv).astype(BF16)
    outs, lses = [], []
    for g, (win, dil) in enumerate(B_GROUPS):
        cs = slice(g * D_MODEL, (g + 1) * D_MODEL)
        o, lse = _attn_group(streams[g], g, win, dil, wq_all[:, cs], wkv_all[:, :att_w][:, cs],
                             wkv_all[:, att_w:][:, cs], rel_bias)
        outs.append(o)
        lses.append(lse)
    x = _merge_groups(x, outs, lses, b_w_out[0])
    return _conv_ffn(x, f_norm_g[1], f_w_up[1], f_conv_w[1], f_conv_b[1], f_w_down[1],
                     final_norm_g, final_norm=True)
```

```python
import functools
import math

import numpy as np
import jax
import jax.numpy as jnp
from jax import lax
from jax.experimental import pallas as pl
from jax.experimental.pallas import tpu as pltpu

F32 = jnp.float32
BF16 = jnp.bfloat16

D_MODEL = 1024
A_HEADS = 4
A_QK_DIM = 128
A_V_DIM = 256
A_NQ = A_HEADS * A_QK_DIM
A_NV = A_HEADS * A_V_DIM
A_Z_DIM = 2 * A_NQ + 2 * A_NV
GATE_SOFTCAP = 15.0
B_GROUPS = ((128, 1), (512, 4), (2048, 16))
B_HEAD_DIM = 64
B_HEADS = 16
B_BLOCK = 128
REL_BUCKETS = 32
REL_MAX_DIST = 2048
D_FF = 2816
EPS = 1e-6
MASK_VALUE = -1e30

LANES = 128
GATE_LANES = LANES
GATE_ROWS = 16
MLSTM_CHUNK = 256
FFN_ROWS = 512
FFN_COLS = 256
CONV_HALO = 8
ATT_ROWS = 512
MERGE_ROWS = 512
VMEM_LIMIT = 56 * 1024 * 1024


def _rms_scale(x):
    return lax.rsqrt(jnp.mean(x * x, axis=-1, keepdims=True) + EPS)


def _softcap(z):
    return GATE_SOFTCAP * jnp.tanh(z / GATE_SOFTCAP)


def _log_sigmoid(a):
    return jnp.minimum(a, 0.0) - jnp.log1p(jnp.exp(-jnp.abs(a)))


def _split3(v):
    hi = v.astype(BF16)
    r1 = v - hi.astype(F32)
    mid = r1.astype(BF16)
    lo = (r1 - mid.astype(F32)).astype(BF16)
    return hi, mid, lo


def _dot(a, b):
    return jnp.dot(a, b, preferred_element_type=F32)


def _dot_nt(a, b):
    return lax.dot_general(a, b, (((1,), (1,)), ((), ())), preferred_element_type=F32)


def _dot_tn(a, b):
    return lax.dot_general(a, b, (((0,), (0,)), ((), ())), preferred_element_type=F32)


def _const_spec(shape):
    return pl.BlockSpec(shape, lambda *_: (0,) * len(shape), pipeline_mode=pl.Buffered(1))


def _mlstm_kernel(x_ref, g_ref, wz_ref, wgc_ref, wgr_ref, bc_ref, br_ref, hg_ref, wo_ref,
                  out_ref, c_ref, n_ref, m_ref, hcat_ref):
    NB, L = x_ref.shape[0], x_ref.shape[1]

    @pl.when(pl.program_id(0) == 0)
    def _():
        c_ref[...] = jnp.zeros_like(c_ref)
        n_ref[...] = jnp.zeros_like(n_ref)
        m_ref[...] = jnp.zeros_like(m_ref)

    row = lax.broadcasted_iota(jnp.int32, (L, L), 0)
    col = lax.broadcasted_iota(jnp.int32, (L, L), 1)
    causal = col <= row
    tril = jnp.where(causal, 1.0, 0.0).astype(BF16)
    triu = jnp.where(row <= col, 1.0, 0.0).astype(BF16)

    proj = []
    for b in range(NB):
        x = x_ref[b]
        xn = (x * _rms_scale(x) * g_ref[...]).astype(BF16)
        z = _dot(xn, wz_ref[...])
        ac = _softcap(_dot(xn, wgc_ref[...]) + bc_ref[...])
        ar = _softcap(_dot_nt(wgr_ref[...], xn) + br_ref[...])
        bcs = sum(_dot(tril, p) for p in _split3(_log_sigmoid(ac)))
        brs = sum(_dot(p, triu) for p in _split3(_log_sigmoid(ar)))
        proj.append((z, ac, ar, bcs, brs))

    for h, b in [(h, b) for h in range(A_HEADS) for b in range(NB)]:
        z, ac, ar, bcs, brs = proj[b]
        st = b * A_HEADS + h
        q = z[:, h * A_QK_DIM:(h + 1) * A_QK_DIM] * (A_QK_DIM ** -0.5)
        k = z[:, A_NQ + h * A_QK_DIM:A_NQ + (h + 1) * A_QK_DIM]
        v = z[:, 2 * A_NQ + h * A_V_DIM:2 * A_NQ + (h + 1) * A_V_DIM]
        o = z[:, 2 * A_NQ + A_NV + h * A_V_DIM:2 * A_NQ + A_NV + (h + 1) * A_V_DIM]
        qb, kb, vb = q.astype(BF16), k.astype(BF16), v.astype(BF16)
        li_c = ac[:, h:h + 1]
        li_r = ar[h:h + 1, :]
        b_c = bcs[:, A_HEADS + h:A_HEADS + h + 1]
        b_r = brs[A_HEADS + h:A_HEADS + h + 1, :]
        m_prev = m_ref[st]
        c_prev = c_ref[st]
        n_prev = n_ref[st]

        log_d = jnp.where(causal, b_c - b_r + li_r, -jnp.inf)
        m_inter = b_c + m_prev
        m_t = jnp.maximum(m_inter, jnp.max(log_d, axis=-1, keepdims=True))
        sm = _dot_nt(qb, kb) * jnp.exp(log_d - m_t)
        w_inter = jnp.exp(m_inter - m_t)
        num = _dot(sm.astype(BF16), vb) + w_inter * _dot(qb, c_prev.astype(BF16))
        den = (jnp.sum(sm, axis=-1, keepdims=True)
               + w_inter * jnp.sum(q * n_prev, axis=-1, keepdims=True))
        hv = num * (1.0 / jnp.maximum(jnp.abs(den), jnp.exp(-m_t)))
        hv = hv * _rms_scale(hv) * hg_ref[:, h * A_V_DIM:(h + 1) * A_V_DIM]
        hcat_ref[b, :, h * A_V_DIM:(h + 1) * A_V_DIM] = (hv * jax.nn.sigmoid(o)).astype(BF16)

        b_last = b_c[L - 1:L, :]
        g_c = b_last - b_c + li_c
        g_r = b_last - b_r + li_r
        m_new = jnp.maximum(b_last + m_prev, jnp.max(g_r, axis=-1, keepdims=True))
        decay = jnp.exp(b_last + m_prev - m_new)
        kw = k * jnp.exp(g_c - m_new)
        c_ref[st] = decay * c_prev + _dot_tn(kw.astype(BF16), vb)
        n_ref[st] = decay * n_prev + jnp.sum(kw, axis=0, keepdims=True)
        m_ref[st] = m_new

    for b in range(NB):
        out_ref[b] = x_ref[b] + _dot(hcat_ref[b], wo_ref[...])


def _mlstm_layer(x, norm_g, w_in, b_if, hnorm_g, w_out):
    B, S, D = x.shape
    L = MLSTM_CHUNK
    wz = w_in[:, :A_Z_DIM].astype(BF16)
    wg = w_in[:, A_Z_DIM:]
    ng = 2 * A_HEADS
    wgc = jnp.pad(wg, ((0, 0), (0, GATE_LANES - ng))).astype(BF16)
    wgr = jnp.pad(wg.T, ((0, GATE_ROWS - ng), (0, 0))).astype(BF16)
    bc = jnp.pad(b_if[None, :], ((0, 0), (0, GATE_LANES - ng)))
    br = jnp.pad(b_if[:, None], ((0, GATE_ROWS - ng), (0, 0)))
    row_spec = pl.BlockSpec((B, L, D), lambda c: (0, c, 0))
    return pl.pallas_call(
        _mlstm_kernel,
        out_shape=jax.ShapeDtypeStruct((B, S, D), F32),
        grid=(S // L,),
        in_specs=[row_spec, _const_spec((1, D)), _const_spec((D, A_Z_DIM)),
                  _const_spec((D, GATE_LANES)), _const_spec((GATE_ROWS, D)),
                  _const_spec((1, GATE_LANES)), _const_spec((GATE_ROWS, 1)),
                  _const_spec((1, A_NV)), _const_spec((A_NV, D))],
        out_specs=row_spec,
        scratch_shapes=[pltpu.VMEM((B * A_HEADS, A_QK_DIM, A_V_DIM), F32),
                        pltpu.VMEM((B * A_HEADS, 1, A_QK_DIM), F32),
                        pltpu.VMEM((B * A_HEADS, 1, 1), F32),
                        pltpu.VMEM((B, L, A_NV), BF16)],
        compiler_params=pltpu.CompilerParams(
            dimension_semantics=("arbitrary",), vmem_limit_bytes=VMEM_LIMIT),
        name="mlstm_layer",
    )(x, norm_g[None, :], wz, wgc, wgr, bc, br, hnorm_g.reshape(1, A_NV), w_out.astype(BF16))


def _ffn_kernel(x_ref, g_ref, wup_ref, cw_ref, cb_ref, wdn_ref, fg_ref, *rest, final_norm):
    if final_norm:
        out_ref, ubuf_ref, act_ref, slab_ref = rest
    else:
        out_ref, xh0_ref, xh1_ref, xh2_ref, ubuf_ref, act_ref, slab_ref = rest
    T = x_ref.shape[0]
    H = CONV_HALO
    half = T // 2
    nslab = D_MODEL // LANES

    @pl.when(pl.program_id(1) == 0)
    def _():
        ubuf_ref[:, 0:H, :] = jnp.zeros((2 * D_FF // LANES, H, LANES), F32)

    x = x_ref[...]
    xn = (x * _rms_scale(x) * g_ref[...]).astype(BF16)
    cw = 2 * FFN_COLS
    for j in range(2 * D_FF // cw):
        u = _dot(xn, wup_ref[:, j * cw:(j + 1) * cw])
        for s in range(cw // LANES):
            ubuf_ref[j * (cw // LANES) + s, H:H + T, :] = u[:, s * LANES:(s + 1) * LANES]

    def conv(k, parity):
        cs = slice(k * LANES, (k + 1) * LANES)
        taps = [ubuf_ref[k, pl.ds(H + parity - 2 + i, half, stride=2), :] for i in range(3)]
        return (taps[0] * cw_ref[0:1, cs] + taps[1] * cw_ref[1:2, cs] + taps[2] * cw_ref[2:3, cs]
                + cb_ref[:, cs])

    for j in range(D_FF // LANES):
        for parity in range(2):
            gate = conv(j, parity)
            val = conv(D_FF // LANES + j, parity)
            act_ref[parity * half:(parity + 1) * half, j * LANES:(j + 1) * LANES] = (
                gate * jax.nn.sigmoid(gate) * val).astype(BF16)

    ubuf_ref[:, 0:H, :] = ubuf_ref[:, T:T + H, :]
    down = _dot(act_ref[...], wdn_ref[...])
    for k in range(nslab):
        for parity in range(2):
            slab_ref[k, pl.ds(parity, half, stride=2), :] = (
                down[parity * half:(parity + 1) * half, k * LANES:(k + 1) * LANES])
    y = x + jnp.concatenate([slab_ref[k] for k in range(nslab)], axis=1)
    if final_norm:
        out_ref[...] = y * _rms_scale(y) * fg_ref[...]
        return
    out_ref[...] = y
    xh = y * _rms_scale(y)
    xh0_ref[...] = xh.astype(BF16)
    for k in range(nslab):
        slab_ref[k] = xh[:, k * LANES:(k + 1) * LANES]
    for dil, ref in ((B_GROUPS[1][1], xh1_ref), (B_GROUPS[2][1], xh2_ref)):
        for r in range(dil):
            for k in range(nslab):
                c0 = r * D_MODEL + k * LANES
                ref[:, c0:c0 + LANES] = slab_ref[k, pl.ds(r, T // dil, stride=dil), :].astype(BF16)


def _conv_ffn(x, norm_g, w_up, conv_w, conv_b, w_down, final_g, final_norm):
    B, S, D = x.shape
    T = FFN_ROWS
    row_spec = pl.BlockSpec((None, T, D), lambda b, t: (b, t, 0))
    out_shape = [jax.ShapeDtypeStruct((B, S, D), F32)]
    out_specs = [row_spec]
    scratch = [pltpu.VMEM((2 * D_FF // LANES, T + CONV_HALO, LANES), F32),
               pltpu.VMEM((T, D_FF), BF16), pltpu.VMEM((D // LANES, T, LANES), F32)]
    if not final_norm:
        out_shape.append(jax.ShapeDtypeStruct((B, S, D), BF16))
        out_specs.append(row_spec)
        for _, dil in B_GROUPS[1:]:
            out_shape.append(jax.ShapeDtypeStruct((B, S // dil, dil * D), BF16))
            out_specs.append(pl.BlockSpec((None, T // dil, dil * D), lambda b, t: (b, t, 0)))
    res = pl.pallas_call(
        functools.partial(_ffn_kernel, final_norm=final_norm),
        out_shape=out_shape,
        grid=(B, S // T),
        in_specs=[row_spec, _const_spec((1, D)), _const_spec((D, 2 * D_FF)),
                  _const_spec((3, 2 * D_FF)), _const_spec((1, 2 * D_FF)),
                  _const_spec((D_FF, D)), _const_spec((1, D))],
        out_specs=out_specs,
        scratch_shapes=scratch,
        compiler_params=pltpu.CompilerParams(
            dimension_semantics=("arbitrary", "arbitrary"), vmem_limit_bytes=VMEM_LIMIT),
        name="conv_ffn_final" if final_norm else "conv_ffn",
    )(x, norm_g[None, :], w_up.astype(BF16), conv_w, conv_b[None, :], w_down.astype(BF16),
      final_g[None, :])
    return res[0] if final_norm else res


def _attn_kernel(xh_ref, wq_ref, wkt_ref, wv_ref, bias_ref, o_ref, lse_ref,
                 q2_ref, kt_ref, v_ref):
    T = xh_ref.shape[0]
    P = B_BLOCK
    i = pl.program_id(2)

    @pl.when(i == 0)
    def _():
        kt_ref[:, 0:P] = jnp.zeros((D_MODEL, P), BF16)
        v_ref[0:P, :] = jnp.zeros((P, D_MODEL), BF16)

    @pl.when(i > 0)
    def _():
        kt_ref[:, 0:P] = kt_ref[:, T:T + P]
        v_ref[0:P, :] = v_ref[T:T + P, :]

    xh = xh_ref[...]
    lane = lax.broadcasted_iota(jnp.int32, (P, D_MODEL), 1)
    even_head = (lane % (2 * B_HEAD_DIM)) < B_HEAD_DIM
    q = _dot(xh, wq_ref[...]) * (B_HEAD_DIM ** -0.5)
    for j in range(T // P):
        qj = q[j * P:(j + 1) * P]
        q2_ref[j, 0:P, :] = jnp.where(even_head, qj, 0.0).astype(BF16)
        q2_ref[j, P:2 * P, :] = jnp.where(even_head, 0.0, qj).astype(BF16)
    kt_ref[:, P:P + T] = _dot_nt(wkt_ref[...], xh).astype(BF16)
    v_ref[P:P + T, :] = _dot(xh, wv_ref[...]).astype(BF16)

    lane2 = lax.broadcasted_iota(jnp.int32, (P, 2 * B_HEAD_DIM), 1)
    first_half = lane2 < B_HEAD_DIM
    ones = jnp.ones((2 * P, LANES), BF16)
    first = (i == 0).astype(jnp.int32)

    for j in range(T // P):
        lse_all = jnp.zeros((P, LANES), F32)
        for hp in range(B_HEADS // 2):
            cs = slice(hp * 2 * B_HEAD_DIM, (hp + 1) * 2 * B_HEAD_DIM)
            s = (_dot(q2_ref[j, :, cs], kt_ref[cs, j * P:(j + 2) * P])
                 + bias_ref[first if j == 0 else 0, hp])
            m = jnp.max(s, axis=-1, keepdims=True)
            p = jnp.exp(s - m).astype(BF16)
            va = jnp.concatenate([v_ref[j * P:(j + 2) * P, cs], ones], axis=1)
            pv = _dot(p, va)
            l = pv[:, LANES:]
            o = pv[:, :LANES] * (1.0 / l)
            lse = m + jnp.log(l)
            o_ref[j * P:(j + 1) * P, cs] = jnp.where(first_half, o[0:P], o[P:2 * P]).astype(BF16)
            lse_all = jnp.where(lane2 == 2 * hp, lse[0:P],
                                jnp.where(lane2 == 2 * hp + 1, lse[P:2 * P], lse_all))
        lse_ref[j * P:(j + 1) * P, :] = lse_all


def _t5_bucket(dist):
    max_exact = REL_BUCKETS // 2
    d = np.maximum(dist, 0)
    log_ratio = np.log(np.maximum(d, 1) / max_exact) / math.log(REL_MAX_DIST / max_exact)
    large = np.minimum(max_exact + (log_ratio * (REL_BUCKETS - max_exact)).astype(np.int64),
                       REL_BUCKETS - 1)
    return np.where(d < max_exact, d, large).astype(np.int32)


def _band_bias(rel_bias, g, win, dil):
    P = B_BLOCK
    n = 3 * P
    t = np.arange(n)
    delta = P - np.where(t < 2 * P, t, t - n)
    valid = (delta >= 0) & (delta <= win // dil)
    vec = rel_bias[_t5_bucket(delta * dil)][:, g * B_HEADS:(g + 1) * B_HEADS].astype(F32)
    vec = jnp.where(valid[:, None], vec, MASK_VALUE).T
    bias = jnp.tile(vec, (1, P))[:, :P * (n - 1)].reshape(B_HEADS, P, n - 1)[:, :, :2 * P]
    bias = bias.reshape(B_HEADS // 2, 2 * P, 2 * P)
    no_prev = jnp.where(np.arange(2 * P) < P, MASK_VALUE, bias)
    return jnp.stack([bias, no_prev])


def _attn_group(xh, g, win, dil, wq, wk, wv, rel_bias):
    B, n, _ = xh.shape
    D = D_MODEL
    T = ATT_ROWS
    cls_spec = pl.BlockSpec((None, T, D), lambda b, r, i: (b, i, r))
    lse_spec = pl.BlockSpec((None, T, LANES), lambda b, r, i: (b, i, r))
    return pl.pallas_call(
        _attn_kernel,
        out_shape=[jax.ShapeDtypeStruct((B, n, dil * D), BF16),
                   jax.ShapeDtypeStruct((B, n, dil * LANES), F32)],
        grid=(B, dil, n // T),
        in_specs=[cls_spec, _const_spec((D, D)), _const_spec((D, D)), _const_spec((D, D)),
                  _const_spec((2, B_HEADS // 2, 2 * B_BLOCK, 2 * B_BLOCK))],
        out_specs=[cls_spec, lse_spec],
        scratch_shapes=[pltpu.VMEM((T // B_BLOCK, 2 * B_BLOCK, D), BF16),
                        pltpu.VMEM((D, T + B_BLOCK), BF16), pltpu.VMEM((T + B_BLOCK, D), BF16)],
        compiler_params=pltpu.CompilerParams(
            dimension_semantics=("arbitrary", "arbitrary", "arbitrary"),
            vmem_limit_bytes=VMEM_LIMIT),
        name=f"dilated_attn_g{g}",
    )(xh, wq, wk.T, wv, _band_bias(rel_bias, g, win, dil))


def _merge_kernel(x_ref, o0_ref, o1_ref, o2_ref, l0_ref, l1_ref, l2_ref, ex_ref, wo_ref, out_ref,
                  os1_ref, os2_ref, ls1_ref, ls2_ref, mg_ref):
    T = x_ref.shape[0]
    nslab = D_MODEL // LANES
    for dil, o_src, o_dst, l_src, l_dst in ((B_GROUPS[1][1], o1_ref, os1_ref, l1_ref, ls1_ref),
                                            (B_GROUPS[2][1], o2_ref, os2_ref, l2_ref, ls2_ref)):
        for r in range(dil):
            rows = pl.ds(r, T // dil, stride=dil)
            l_dst[rows, :] = l_src[:, r * LANES:(r + 1) * LANES]
            for k in range(nslab):
                c0 = r * D_MODEL + k * LANES
                o_dst[k, rows, :] = o_src[:, c0:c0 + LANES].astype(F32)
    l0, l1, l2 = l0_ref[...], ls1_ref[...], ls2_ref[...]
    m = jnp.maximum(jnp.maximum(l0, l1), l2)
    e0, e1, e2 = jnp.exp(l0 - m), jnp.exp(l1 - m), jnp.exp(l2 - m)
    inv = 1.0 / (e0 + e1 + e2)

    def spread(w):
        return _dot(w.astype(BF16), ex_ref[...])

    w0, w1, w2 = spread(e0 * inv), spread(e1 * inv), spread(e2 * inv)
    for k in range(nslab):
        ks = slice(k * LANES, (k + 1) * LANES)
        mg = w0[:, ks] * o0_ref[:, ks].astype(F32) + w1[:, ks] * os1_ref[k] + w2[:, ks] * os2_ref[k]
        mg_ref[:, ks] = mg.astype(BF16)
    out_ref[...] = x_ref[...] + _dot(mg_ref[...], wo_ref[...])


def _merge_groups(x, outs, lses, w_out):
    B, S, D = x.shape
    T = MERGE_ROWS
    row_spec = pl.BlockSpec((None, T, D), lambda b, t: (b, t, 0))
    o_specs = [pl.BlockSpec((None, T // dil, dil * D), lambda b, t: (b, t, 0)) for _, dil in B_GROUPS]
    l_specs = [pl.BlockSpec((None, T // dil, dil * LANES), lambda b, t: (b, t, 0))
               for _, dil in B_GROUPS]
    expand = np.zeros((LANES, D), np.float32)
    for h in range(B_HEADS):
        expand[h, h * B_HEAD_DIM:(h + 1) * B_HEAD_DIM] = 1.0
    nslab = D // LANES
    return pl.pallas_call(
        _merge_kernel,
        out_shape=jax.ShapeDtypeStruct((B, S, D), F32),
        grid=(B, S // T),
        in_specs=[row_spec] + o_specs + l_specs + [_const_spec((LANES, D)), _const_spec((D, D))],
        out_specs=row_spec,
        scratch_shapes=[pltpu.VMEM((nslab, T, LANES), F32), pltpu.VMEM((nslab, T, LANES), F32),
                        pltpu.VMEM((T, LANES), F32), pltpu.VMEM((T, LANES), F32),
                        pltpu.VMEM((T, D), BF16)],
        compiler_params=pltpu.CompilerParams(
            dimension_semantics=("arbitrary", "arbitrary"), vmem_limit_bytes=VMEM_LIMIT),
        name="merge_groups",
    )(x, *outs, *lses, jnp.asarray(expand, BF16), w_out.astype(BF16))


def kernel(x, a_norm_g, a_w_in, a_b_if, a_hnorm_g, a_w_out, kv_norm_g, w_kv, b_norm_g, b_w_q,
           b_w_out, rel_bias, f_norm_g, f_w_up, f_conv_w, f_conv_b, f_w_down, final_norm_g):
    x = _mlstm_layer(x, a_norm_g[0], a_w_in[0], a_b_if[0], a_hnorm_g[0], a_w_out[0])
    x, *streams = _conv_ffn(x, f_norm_g[0], f_w_up[0], f_conv_w[0], f_conv_b[0], f_w_down[0],
                            final_norm_g, final_norm=False)
    att_w = len(B_GROUPS) * B_HEADS * B_HEAD_DIM
    wq_all = (b_norm_g[0][:, None] * b_w_q[0]).astype(BF16)
    wkv_all = (kv_norm_g[:, None] * w_kv).astype(BF16)
    outs, lses = [], []
    for g, (win, dil) in enumerate(B_GROUPS):
        cs = slice(g * D_MODEL, (g + 1) * D_MODEL)
        o, lse = _attn_group(streams[g], g, win, dil, wq_all[:, cs], wkv_all[:, :att_w][:, cs],
                             wkv_all[:, att_w:][:, cs], rel_bias)
        outs.append(o)
        lses.append(lse)
    x = _merge_groups(x, outs, lses, b_w_out[0])
    return _conv_ffn(x, f_norm_g[1], f_w_up[1], f_conv_w[1], f_conv_b[1], f_w_down[1],
                     final_norm_g, final_norm=True)
```

```python
import functools
import math

import numpy as np
import jax
import jax.numpy as jnp
from jax import lax
from jax.experimental import pallas as pl
from jax.experimental.pallas import tpu as pltpu

F32 = jnp.float32
BF16 = jnp.bfloat16

D_MODEL = 1024
A_HEADS = 4
A_QK_DIM = 128
A_V_DIM = 256
A_NQ = A_HEADS * A_QK_DIM
A_NV = A_HEADS * A_V_DIM
A_Z_DIM = 2 * A_NQ + 2 * A_NV
GATE_SOFTCAP = 15.0
B_GROUPS = ((128, 1), (512, 4), (2048, 16))
B_HEAD_DIM = 64
B_HEADS = 16
B_BLOCK = 128
REL_BUCKETS = 32
REL_MAX_DIST = 2048
D_FF = 2816
EPS = 1e-6
MASK_VALUE = -1e30

LANES = 128
GATE_LANES = LANES
GATE_ROWS = 16
MLSTM_CHUNK = 256
FFN_ROWS = 512
FFN_COLS = 256
CONV_HALO = 8
ATT_ROWS = 1024
MERGE_ROWS = 1024
VMEM_LIMIT = 56 * 1024 * 1024


def _rms_scale(x):
    return lax.rsqrt(jnp.mean(x * x, axis=-1, keepdims=True) + EPS)


def _softcap(z):
    return GATE_SOFTCAP * jnp.tanh(z / GATE_SOFTCAP)


def _log_sigmoid(a):
    return jnp.minimum(a, 0.0) - jnp.log1p(jnp.exp(-jnp.abs(a)))


def _split3(v):
    hi = v.astype(BF16)
    r1 = v - hi.astype(F32)
    mid = r1.astype(BF16)
    lo = (r1 - mid.astype(F32)).astype(BF16)
    return hi, mid, lo


def _dot(a, b):
    return jnp.dot(a, b, preferred_element_type=F32)


def _dot_nt(a, b):
    return lax.dot_general(a, b, (((1,), (1,)), ((), ())), preferred_element_type=F32)


def _dot_tn(a, b):
    return lax.dot_general(a, b, (((0,), (0,)), ((), ())), preferred_element_type=F32)


def _const_spec(shape):
    return pl.BlockSpec(shape, lambda *_: (0,) * len(shape), pipeline_mode=pl.Buffered(1))


def _mlstm_kernel(x_ref, g_ref, wz_ref, wgr_ref, bc_ref, br_ref, hg_ref, wo_ref,
                  out_ref, c_ref, n_ref, m_ref, hcat_ref):
    NB, L = x_ref.shape[0], x_ref.shape[1]

    @pl.when(pl.program_id(0) == 0)
    def _():
        c_ref[...] = jnp.zeros_like(c_ref)
        n_ref[...] = jnp.zeros_like(n_ref)
        m_ref[...] = jnp.zeros_like(m_ref)

    row = lax.broadcasted_iota(jnp.int32, (L, L), 0)
    col = lax.broadcasted_iota(jnp.int32, (L, L), 1)
    causal = col <= row
    tril = jnp.where(causal, 1.0, 0.0).astype(BF16)
    triu = jnp.where(row <= col, 1.0, 0.0).astype(BF16)

    proj = []
    for b in range(NB):
        x = x_ref[b]
        xn = (x * _rms_scale(x) * g_ref[...]).astype(BF16)
        z = _dot(xn, wz_ref[...])
        ac = _softcap(z[:, A_Z_DIM:] + bc_ref[...])
        ar = _softcap(_dot_nt(wgr_ref[...], xn) + br_ref[...])
        bcs = sum(_dot(tril, p) for p in _split3(_log_sigmoid(ac)))
        brs = sum(_dot(p, triu) for p in _split3(_log_sigmoid(ar)))
        proj.append((z, ac, ar, bcs, brs))

    for h, b in [(h, b) for h in range(A_HEADS) for b in range(NB)]:
        z, ac, ar, bcs, brs = proj[b]
        st = b * A_HEADS + h
        q = z[:, h * A_QK_DIM:(h + 1) * A_QK_DIM] * (A_QK_DIM ** -0.5)
        k = z[:, A_NQ + h * A_QK_DIM:A_NQ + (h + 1) * A_QK_DIM]
        v = z[:, 2 * A_NQ + h * A_V_DIM:2 * A_NQ + (h + 1) * A_V_DIM]
        o = z[:, 2 * A_NQ + A_NV + h * A_V_DIM:2 * A_NQ + A_NV + (h + 1) * A_V_DIM]
        qb, kb, vb = q.astype(BF16), k.astype(BF16), v.astype(BF16)
        li_c = ac[:, h:h + 1]
        li_r = ar[h:h + 1, :]
        b_c = bcs[:, A_HEADS + h:A_HEADS + h + 1]
        b_r = brs[A_HEADS + h:A_HEADS + h + 1, :]
        m_prev = m_ref[st]
        c_prev = c_ref[st]
        n_prev = n_ref[st]

        log_d = jnp.where(causal, b_c - b_r + li_r, -jnp.inf)
        m_inter = b_c + m_prev
        m_t = jnp.maximum(m_inter, jnp.max(log_d, axis=-1, keepdims=True))
        sm = _dot_nt(qb, kb) * jnp.exp(log_d - m_t)
        w_inter = jnp.exp(m_inter - m_t)
        num = _dot(sm.astype(BF16), vb) + w_inter * _dot(qb, c_prev.astype(BF16))
        den = (jnp.sum(sm, axis=-1, keepdims=True)
               + w_inter * jnp.sum(q * n_prev, axis=-1, keepdims=True))
        hv = num * (1.0 / jnp.maximum(jnp.abs(den), jnp.exp(-m_t)))
        hv = hv * _rms_scale(hv) * hg_ref[:, h * A_V_DIM:(h + 1) * A_V_DIM]
        hcat_ref[b, :, h * A_V_DIM:(h + 1) * A_V_DIM] = (hv * jax.nn.sigmoid(o)).astype(BF16)

        b_last = b_c[L - 1:L, :]
        g_c = b_last - b_c + li_c
        g_r = b_last - b_r + li_r
        m_new = jnp.maximum(b_last + m_prev, jnp.max(g_r, axis=-1, keepdims=True))
        decay = jnp.exp(b_last + m_prev - m_new)
        kw = k * jnp.exp(g_c - m_new)
        c_ref[st] = decay * c_prev + _dot_tn(kw.astype(BF16), vb)
        n_ref[st] = decay * n_prev + jnp.sum(kw, axis=0, keepdims=True)
        m_ref[st] = m_new

    for b in range(NB):
        out_ref[b] = x_ref[b] + _dot(hcat_ref[b], wo_ref[...])


def _mlstm_layer(x, norm_g, w_in, b_if, hnorm_g, w_out):
    B, S, D = x.shape
    L = MLSTM_CHUNK
    wg = w_in[:, A_Z_DIM:]
    ng = 2 * A_HEADS
    wz = jnp.pad(w_in, ((0, 0), (0, GATE_LANES - ng))).astype(BF16)
    wgr = jnp.pad(wg.T, ((0, GATE_ROWS - ng), (0, 0))).astype(BF16)
    bc = jnp.pad(b_if[None, :], ((0, 0), (0, GATE_LANES - ng)))
    br = jnp.pad(b_if[:, None], ((0, GATE_ROWS - ng), (0, 0)))
    row_spec = pl.BlockSpec((B, L, D), lambda c: (0, c, 0))
    return pl.pallas_call(
        _mlstm_kernel,
        out_shape=jax.ShapeDtypeStruct((B, S, D), F32),
        grid=(S // L,),
        in_specs=[row_spec, _const_spec((1, D)), _const_spec((D, A_Z_DIM + GATE_LANES)),
                  _const_spec((GATE_ROWS, D)),
                  _const_spec((1, GATE_LANES)), _const_spec((GATE_ROWS, 1)),
                  _const_spec((1, A_NV)), _const_spec((A_NV, D))],
        out_specs=row_spec,
        scratch_shapes=[pltpu.VMEM((B * A_HEADS, A_QK_DIM, A_V_DIM), F32),
                        pltpu.VMEM((B * A_HEADS, 1, A_QK_DIM), F32),
                        pltpu.VMEM((B * A_HEADS, 1, 1), F32),
                        pltpu.VMEM((B, L, A_NV), BF16)],
        compiler_params=pltpu.CompilerParams(
            dimension_semantics=("arbitrary",), vmem_limit_bytes=VMEM_LIMIT),
        name="mlstm_layer",
    )(x, norm_g[None, :], wz, wgr, bc, br, hnorm_g.reshape(1, A_NV), w_out.astype(BF16))


def _ffn_kernel(x_ref, g_ref, wup_ref, cw_ref, cb_ref, wdn_ref, fg_ref, *rest, final_norm):
    if final_norm:
        out_ref, ubuf_ref, act_ref, slab_ref = rest
    else:
        out_ref, xh0_ref, xh1_ref, xh2_ref, ubuf_ref, act_ref, slab_ref = rest
    T = x_ref.shape[0]
    H = CONV_HALO
    half = T // 2
    nslab = D_MODEL // LANES

    @pl.when(pl.program_id(1) == 0)
    def _():
        ubuf_ref[:, 0:H, :] = jnp.zeros((2 * D_FF // LANES, H, LANES), F32)

    x = x_ref[...]
    xn = (x * _rms_scale(x) * g_ref[...]).astype(BF16)
    cw = 2 * FFN_COLS
    for j in range(2 * D_FF // cw):
        u = _dot(xn, wup_ref[:, j * cw:(j + 1) * cw])
        for s in range(cw // LANES):
            ubuf_ref[j * (cw // LANES) + s, H:H + T, :] = u[:, s * LANES:(s + 1) * LANES]

    def conv(k, parity):
        cs = slice(k * LANES, (k + 1) * LANES)
        taps = [ubuf_ref[k, pl.ds(H + parity - 2 + i, half, stride=2), :] for i in range(3)]
        return (taps[0] * cw_ref[0:1, cs] + taps[1] * cw_ref[1:2, cs] + taps[2] * cw_ref[2:3, cs]
                + cb_ref[:, cs])

    for j in range(D_FF // LANES):
        for parity in range(2):
            gate = conv(j, parity)
            val = conv(D_FF // LANES + j, parity)
            act_ref[parity * half:(parity + 1) * half, j * LANES:(j + 1) * LANES] = (
                gate * jax.nn.sigmoid(gate) * val).astype(BF16)

    ubuf_ref[:, 0:H, :] = ubuf_ref[:, T:T + H, :]
    down = _dot(act_ref[...], wdn_ref[...])
    for k in range(nslab):
        for parity in range(2):
            slab_ref[k, pl.ds(parity, half, stride=2), :] = (
                down[parity * half:(parity + 1) * half, k * LANES:(k + 1) * LANES])
    y = x + jnp.concatenate([slab_ref[k] for k in range(nslab)], axis=1)
    if final_norm:
        out_ref[...] = y * _rms_scale(y) * fg_ref[...]
        return
    out_ref[...] = y
    xh = y * _rms_scale(y)
    xh0_ref[...] = xh.astype(BF16)
    for k in range(nslab):
        slab_ref[k] = xh[:, k * LANES:(k + 1) * LANES]
    for dil, ref in ((B_GROUPS[1][1], xh1_ref), (B_GROUPS[2][1], xh2_ref)):
        for r in range(dil):
            for k in range(nslab):
                c0 = r * D_MODEL + k * LANES
                ref[:, c0:c0 + LANES] = slab_ref[k, pl.ds(r, T // dil, stride=dil), :].astype(BF16)


def _conv_ffn(x, norm_g, w_up, conv_w, conv_b, w_down, final_g, final_norm):
    B, S, D = x.shape
    T = FFN_ROWS
    row_spec = pl.BlockSpec((None, T, D), lambda b, t: (b, t, 0))
    out_shape = [jax.ShapeDtypeStruct((B, S, D), F32)]
    out_specs = [row_spec]
    scratch = [pltpu.VMEM((2 * D_FF // LANES, T + CONV_HALO, LANES), F32),
               pltpu.VMEM((T, D_FF), BF16), pltpu.VMEM((D // LANES, T, LANES), F32)]
    if not final_norm:
        out_shape.append(jax.ShapeDtypeStruct((B, S, D), BF16))
        out_specs.append(row_spec)
        for _, dil in B_GROUPS[1:]:
            out_shape.append(jax.ShapeDtypeStruct((B, S // dil, dil * D), BF16))
            out_specs.append(pl.BlockSpec((None, T // dil, dil * D), lambda b, t: (b, t, 0)))
    res = pl.pallas_call(
        functools.partial(_ffn_kernel, final_norm=final_norm),
        out_shape=out_shape,
        grid=(B, S // T),
        in_specs=[row_spec, _const_spec((1, D)), _const_spec((D, 2 * D_FF)),
                  _const_spec((3, 2 * D_FF)), _const_spec((1, 2 * D_FF)),
                  _const_spec((D_FF, D)), _const_spec((1, D))],
        out_specs=out_specs,
        scratch_shapes=scratch,
        compiler_params=pltpu.CompilerParams(
            dimension_semantics=("arbitrary", "arbitrary"), vmem_limit_bytes=VMEM_LIMIT),
        name="conv_ffn_final" if final_norm else "conv_ffn",
    )(x, norm_g[None, :], w_up.astype(BF16), conv_w, conv_b[None, :], w_down.astype(BF16),
      final_g[None, :])
    return res[0] if final_norm else res


def _attn_kernel(xh_ref, wq_ref, wkt_ref, wv_ref, bias_ref, o_ref, lse_ref,
                 q2_ref, kt_ref, v_ref):
    T = xh_ref.shape[0]
    P = B_BLOCK
    i = pl.program_id(2)

    @pl.when(i == 0)
    def _():
        kt_ref[:, 0:P] = jnp.zeros((D_MODEL, P), BF16)
        v_ref[0:P, :] = jnp.zeros((P, D_MODEL), BF16)

    @pl.when(i > 0)
    def _():
        kt_ref[:, 0:P] = kt_ref[:, T:T + P]
        v_ref[0:P, :] = v_ref[T:T + P, :]

    xh = xh_ref[...]
    lane = lax.broadcasted_iota(jnp.int32, (P, D_MODEL), 1)
    even_head = (lane % (2 * B_HEAD_DIM)) < B_HEAD_DIM
    q = _dot(xh, wq_ref[...]) * (B_HEAD_DIM ** -0.5)
    for j in range(T // P):
        qj = q[j * P:(j + 1) * P]
        q2_ref[j, 0:P, :] = jnp.where(even_head, qj, 0.0).astype(BF16)
        q2_ref[j, P:2 * P, :] = jnp.where(even_head, 0.0, qj).astype(BF16)
    kt_ref[:, P:P + T] = _dot_nt(wkt_ref[...], xh).astype(BF16)
    v_ref[P:P + T, :] = _dot(xh, wv_ref[...]).astype(BF16)

    lane2 = lax.broadcasted_iota(jnp.int32, (P, 2 * B_HEAD_DIM), 1)
    first_half = lane2 < B_HEAD_DIM
    ones = jnp.ones((2 * P, LANES), BF16)
    first = (i == 0).astype(jnp.int32)

    for j in range(T // P):
        lse_all = jnp.zeros((P, LANES), F32)
        for hp in range(B_HEADS // 2):
            cs = slice(hp * 2 * B_HEAD_DIM, (hp + 1) * 2 * B_HEAD_DIM)
            s = (_dot(q2_ref[j, :, cs], kt_ref[cs, j * P:(j + 2) * P])
                 + bias_ref[first if j == 0 else 0, hp])
            m = jnp.max(s, axis=-1, keepdims=True)
            p = jnp.exp(s - m).astype(BF16)
            va = jnp.concatenate([v_ref[j * P:(j + 2) * P, cs], ones], axis=1)
            pv = _dot(p, va)
            l = pv[:, LANES:]
            o = pv[:, :LANES] * (1.0 / l)
            lse = m + jnp.log(l)
            o_ref[j * P:(j + 1) * P, cs] = jnp.where(first_half, o[0:P], o[P:2 * P]).astype(BF16)
            lse_all = jnp.where(lane2 == 2 * hp, lse[0:P],
                                jnp.where(lane2 == 2 * hp + 1, lse[P:2 * P], lse_all))
        lse_ref[j * P:(j + 1) * P, :] = lse_all


def _t5_bucket(dist):
    max_exact = REL_BUCKETS // 2
    d = np.maximum(dist, 0)
    log_ratio = np.log(np.maximum(d, 1) / max_exact) / math.log(REL_MAX_DIST / max_exact)
    large = np.minimum(max_exact + (log_ratio * (REL_BUCKETS - max_exact)).astype(np.int64),
                       REL_BUCKETS - 1)
    return np.where(d < max_exact, d, large).astype(np.int32)


def _band_bias(rel_bias, g, win, dil):
    P = B_BLOCK
    n = 3 * P
    t = np.arange(n)
    delta = P - np.where(t < 2 * P, t, t - n)
    valid = (delta >= 0) & (delta <= win // dil)
    vec = rel_bias[_t5_bucket(delta * dil)][:, g * B_HEADS:(g + 1) * B_HEADS].astype(F32)
    vec = jnp.where(valid[:, None], vec, MASK_VALUE).T
    bias = jnp.tile(vec, (1, P))[:, :P * (n - 1)].reshape(B_HEADS, P, n - 1)[:, :, :2 * P]
    bias = bias.reshape(B_HEADS // 2, 2 * P, 2 * P)
    no_prev = jnp.where(np.arange(2 * P) < P, MASK_VALUE, bias)
    return jnp.stack([bias, no_prev])


def _attn_group(xh, g, win, dil, wq_all, wkt_all, wkv_all, rel_bias):
    B, n, _ = xh.shape
    D = D_MODEL
    T = ATT_ROWS
    cls_spec = pl.BlockSpec((None, T, D), lambda b, r, i: (b, i, r))
    lse_spec = pl.BlockSpec((None, T, LANES), lambda b, r, i: (b, i, r))

    def weight_spec(row_blk, col_blk):
        return pl.BlockSpec((D, D), lambda *_: (row_blk, col_blk), pipeline_mode=pl.Buffered(1))

    return pl.pallas_call(
        _attn_kernel,
        out_shape=[jax.ShapeDtypeStruct((B, n, dil * D), BF16),
                   jax.ShapeDtypeStruct((B, n, dil * LANES), F32)],
        grid=(B, dil, n // T),
        in_specs=[cls_spec, weight_spec(0, g), weight_spec(g, 0), weight_spec(0, len(B_GROUPS) + g),
                  _const_spec((2, B_HEADS // 2, 2 * B_BLOCK, 2 * B_BLOCK))],
        out_specs=[cls_spec, lse_spec],
        scratch_shapes=[pltpu.VMEM((T // B_BLOCK, 2 * B_BLOCK, D), BF16),
                        pltpu.VMEM((D, T + B_BLOCK), BF16), pltpu.VMEM((T + B_BLOCK, D), BF16)],
        compiler_params=pltpu.CompilerParams(
            dimension_semantics=("arbitrary", "arbitrary", "arbitrary"),
            vmem_limit_bytes=VMEM_LIMIT),
        name=f"dilated_attn_g{g}",
    )(xh, wq_all, wkt_all, wkv_all, _band_bias(rel_bias, g, win, dil))


def _merge_kernel(x_ref, o0_ref, o1_ref, o2_ref, l0_ref, l1_ref, l2_ref, ex_ref, wo_ref, out_ref,
                  os1_ref, os2_ref, ls1_ref, ls2_ref, mg_ref):
    T = x_ref.shape[0]
    nslab = D_MODEL // LANES
    for dil, o_src, o_dst, l_src, l_dst in ((B_GROUPS[1][1], o1_ref, os1_ref, l1_ref, ls1_ref),
                                            (B_GROUPS[2][1], o2_ref, os2_ref, l2_ref, ls2_ref)):
        for r in range(dil):
            rows = pl.ds(r, T // dil, stride=dil)
            l_dst[rows, :] = l_src[:, r * LANES:(r + 1) * LANES]
            for k in range(nslab):
                c0 = r * D_MODEL + k * LANES
                o_dst[k, rows, :] = o_src[:, c0:c0 + LANES].astype(F32)
    l0, l1, l2 = l0_ref[...], ls1_ref[...], ls2_ref[...]
    m = jnp.maximum(jnp.maximum(l0, l1), l2)
    e0, e1, e2 = jnp.exp(l0 - m), jnp.exp(l1 - m), jnp.exp(l2 - m)
    inv = 1.0 / (e0 + e1 + e2)

    def spread(w):
        return _dot(w.astype(BF16), ex_ref[...])

    w0, w1 = spread(e0 * inv), spread(e1 * inv)
    for k in range(nslab):
        ks = slice(k * LANES, (k + 1) * LANES)
        o2 = os2_ref[k]
        mg = o2 + w0[:, ks] * (o0_ref[:, ks].astype(F32) - o2) + w1[:, ks] * (os1_ref[k] - o2)
        mg_ref[:, ks] = mg.astype(BF16)
    out_ref[...] = x_ref[...] + _dot(mg_ref[...], wo_ref[...])


def _merge_groups(x, outs, lses, w_out):
    B, S, D = x.shape
    T = MERGE_ROWS
    row_spec = pl.BlockSpec((None, T, D), lambda b, t: (b, t, 0))
    o_specs = [pl.BlockSpec((None, T // dil, dil * D), lambda b, t: (b, t, 0)) for _, dil in B_GROUPS]
    l_specs = [pl.BlockSpec((None, T // dil, dil * LANES), lambda b, t: (b, t, 0))
               for _, dil in B_GROUPS]
    expand = np.zeros((LANES, D), np.float32)
    for h in range(B_HEADS):
        expand[h, h * B_HEAD_DIM:(h + 1) * B_HEAD_DIM] = 1.0
    nslab = D // LANES
    return pl.pallas_call(
        _merge_kernel,
        out_shape=jax.ShapeDtypeStruct((B, S, D), F32),
        grid=(B, S // T),
        in_specs=[row_spec] + o_specs + l_specs + [_const_spec((LANES, D)), _const_spec((D, D))],
        out_specs=row_spec,
        scratch_shapes=[pltpu.VMEM((nslab, T, LANES), F32), pltpu.VMEM((nslab, T, LANES), F32),
                        pltpu.VMEM((T, LANES), F32), pltpu.VMEM((T, LANES), F32),
                        pltpu.VMEM((T, D), BF16)],
        compiler_params=pltpu.CompilerParams(
            dimension_semantics=("arbitrary", "arbitrary"), vmem_limit_bytes=VMEM_LIMIT),
        name="merge_groups",
    )(x, *outs, *lses, jnp.asarray(expand, BF16), w_out.astype(BF16))


def kernel(x, a_norm_g, a_w_in, a_b_if, a_hnorm_g, a_w_out, kv_norm_g, w_kv, b_norm_g, b_w_q,
           b_w_out, rel_bias, f_norm_g, f_w_up, f_conv_w, f_conv_b, f_w_down, final_norm_g):
    x = _mlstm_layer(x, a_norm_g[0], a_w_in[0], a_b_if[0], a_hnorm_g[0], a_w_out[0])
    x, *streams = _conv_ffn(x, f_norm_g[0], f_w_up[0], f_conv_w[0], f_conv_b[0], f_w_down[0],
                            final_norm_g, final_norm=False)
    att_w = len(B_GROUPS) * B_HEADS * B_HEAD_DIM
    wq_all = (b_norm_g[0][:, None] * b_w_q[0]).astype(BF16)
    wkv_all = (kv_norm_g[:, None] * w_kv).astype(BF16)
    wkt_all = wkv_all[:, :att_w].T
    outs, lses = [], []
    for g, (win, dil) in enumerate(B_GROUPS):
        o, lse = _attn_group(streams[g], g, win, dil, wq_all, wkt_all, wkv_all, rel_bias)
        outs.append(o)
        lses.append(lse)
    x = _merge_groups(x, outs, lses, b_w_out[0])
    return _conv_ffn(x, f_norm_g[1], f_w_up[1], f_conv_w[1], f_conv_b[1], f_w_down[1],
                     final_norm_g, final_norm=True)
```

```python
import functools
import math

import numpy as np
import jax
import jax.numpy as jnp
from jax import lax
from jax.experimental import pallas as pl
from jax.experimental.pallas import tpu as pltpu

F32 = jnp.float32
BF16 = jnp.bfloat16

D_MODEL = 1024
A_HEADS = 4
A_QK_DIM = 128
A_V_DIM = 256
A_NQ = A_HEADS * A_QK_DIM
A_NV = A_HEADS * A_V_DIM
A_Z_DIM = 2 * A_NQ + 2 * A_NV
GATE_SOFTCAP = 15.0
B_GROUPS = ((128, 1), (512, 4), (2048, 16))
B_HEAD_DIM = 64
B_HEADS = 16
B_BLOCK = 128
REL_BUCKETS = 32
REL_MAX_DIST = 2048
D_FF = 2816
EPS = 1e-6
MASK_VALUE = -1e30

LANES = 128
GATE_LANES = LANES
GATE_ROWS = 16
MLSTM_CHUNK = 256
FFN_ROWS = 512
FFN_COLS = 256
CONV_HALO = 8
ATT_ROWS = 1024
MERGE_ROWS = 1024
VMEM_LIMIT = 56 * 1024 * 1024


def _rms_scale(x):
    return lax.rsqrt(jnp.mean(x * x, axis=-1, keepdims=True) + EPS)


def _softcap(z):
    return GATE_SOFTCAP * jnp.tanh(z / GATE_SOFTCAP)


def _log_sigmoid(a):
    return jnp.minimum(a, 0.0) - jnp.log1p(jnp.exp(-jnp.abs(a)))


def _split3(v):
    hi = v.astype(BF16)
    r1 = v - hi.astype(F32)
    mid = r1.astype(BF16)
    lo = (r1 - mid.astype(F32)).astype(BF16)
    return hi, mid, lo


def _dot(a, b):
    return jnp.dot(a, b, preferred_element_type=F32)


def _dot_nt(a, b):
    return lax.dot_general(a, b, (((1,), (1,)), ((), ())), preferred_element_type=F32)


def _dot_tn(a, b):
    return lax.dot_general(a, b, (((0,), (0,)), ((), ())), preferred_element_type=F32)


def _const_spec(shape):
    return pl.BlockSpec(shape, lambda *_: (0,) * len(shape), pipeline_mode=pl.Buffered(1))


def _mlstm_kernel(x_ref, g_ref, wz_ref, wgr_ref, bc_ref, br_ref, hg_ref, wo_ref,
                  out_ref, c_ref, n_ref, m_ref, hcat_ref):
    NB, L = x_ref.shape[0], x_ref.shape[1]

    @pl.when(pl.program_id(0) == 0)
    def _():
        c_ref[...] = jnp.zeros_like(c_ref)
        n_ref[...] = jnp.zeros_like(n_ref)
        m_ref[...] = jnp.zeros_like(m_ref)

    row = lax.broadcasted_iota(jnp.int32, (L, L), 0)
    col = lax.broadcasted_iota(jnp.int32, (L, L), 1)
    causal = col <= row
    tril = jnp.where(causal, 1.0, 0.0).astype(BF16)
    triu = jnp.where(row <= col, 1.0, 0.0).astype(BF16)

    def projection(b):
        x = x_ref[b]
        xn = (x * _rms_scale(x) * g_ref[...]).astype(BF16)
        z = {}

        def gates():
            z["ac"] = _softcap(_dot(xn, wz_ref[:, A_Z_DIM:]) + bc_ref[...])
            z["ar"] = _softcap(_dot_nt(wgr_ref[...], xn) + br_ref[...])
            z["bcs"] = sum(_dot(tril, p) for p in _split3(_log_sigmoid(z["ac"])))
            z["brs"] = sum(_dot(p, triu) for p in _split3(_log_sigmoid(z["ar"])))

        def columns(name, c0, c1):
            z[name] = _dot(xn, wz_ref[:, c0:c1])

        return z, [gates,
                   functools.partial(columns, "qk", 0, 2 * A_NQ),
                   functools.partial(columns, "v", 2 * A_NQ, 2 * A_NQ + A_NV),
                   functools.partial(columns, "o", 2 * A_NQ + A_NV, A_Z_DIM)]

    def head(h, b, z):
        st = b * A_HEADS + h
        q = z["qk"][:, h * A_QK_DIM:(h + 1) * A_QK_DIM] * (A_QK_DIM ** -0.5)
        k = z["qk"][:, A_NQ + h * A_QK_DIM:A_NQ + (h + 1) * A_QK_DIM]
        v = z["v"][:, h * A_V_DIM:(h + 1) * A_V_DIM]
        o = z["o"][:, h * A_V_DIM:(h + 1) * A_V_DIM]
        qb, kb, vb = q.astype(BF16), k.astype(BF16), v.astype(BF16)
        li_c = z["ac"][:, h:h + 1]
        li_r = z["ar"][h:h + 1, :]
        b_c = z["bcs"][:, A_HEADS + h:A_HEADS + h + 1]
        b_r = z["brs"][A_HEADS + h:A_HEADS + h + 1, :]
        m_prev = m_ref[st]
        c_prev = c_ref[st]
        n_prev = n_ref[st]

        log_d = jnp.where(causal, b_c - b_r + li_r, -jnp.inf)
        m_inter = b_c + m_prev
        m_t = jnp.maximum(m_inter, jnp.max(log_d, axis=-1, keepdims=True))
        sm = _dot_nt(qb, kb) * jnp.exp(log_d - m_t)
        w_inter = jnp.exp(m_inter - m_t)
        num = _dot(sm.astype(BF16), vb) + w_inter * _dot(qb, c_prev.astype(BF16))
        den = (jnp.sum(sm, axis=-1, keepdims=True)
               + w_inter * jnp.sum(q * n_prev, axis=-1, keepdims=True))
        hv = num * (1.0 / jnp.maximum(jnp.abs(den), jnp.exp(-m_t)))
        hv = hv * _rms_scale(hv) * hg_ref[:, h * A_V_DIM:(h + 1) * A_V_DIM]
        hcat_ref[b, :, h * A_V_DIM:(h + 1) * A_V_DIM] = (hv * jax.nn.sigmoid(o)).astype(BF16)

        b_last = b_c[L - 1:L, :]
        g_c = b_last - b_c + li_c
        g_r = b_last - b_r + li_r
        m_new = jnp.maximum(b_last + m_prev, jnp.max(g_r, axis=-1, keepdims=True))
        decay = jnp.exp(b_last + m_prev - m_new)
        kw = k * jnp.exp(g_c - m_new)
        c_ref[st] = decay * c_prev + _dot_tn(kw.astype(BF16), vb)
        n_ref[st] = decay * n_prev + jnp.sum(kw, axis=0, keepdims=True)
        m_ref[st] = m_new

    z_prev = None
    for b in range(NB):
        z, pieces = projection(b)
        for h in range(A_HEADS):
            if z_prev is not None:
                head(h, b - 1, z_prev)
            pieces[h]()
        z_prev = z
    for h in range(A_HEADS):
        head(h, NB - 1, z_prev)

    for b in range(NB):
        out_ref[b] = x_ref[b] + _dot(hcat_ref[b], wo_ref[...])


def _mlstm_layer(x, norm_g, w_in, b_if, hnorm_g, w_out):
    B, S, D = x.shape
    L = MLSTM_CHUNK
    wg = w_in[:, A_Z_DIM:]
    ng = 2 * A_HEADS
    wz = jnp.pad(w_in, ((0, 0), (0, GATE_LANES - ng))).astype(BF16)
    wgr = jnp.pad(wg.T, ((0, GATE_ROWS - ng), (0, 0))).astype(BF16)
    bc = jnp.pad(b_if[None, :], ((0, 0), (0, GATE_LANES - ng)))
    br = jnp.pad(b_if[:, None], ((0, GATE_ROWS - ng), (0, 0)))
    row_spec = pl.BlockSpec((B, L, D), lambda c: (0, c, 0))
    return pl.pallas_call(
        _mlstm_kernel,
        out_shape=jax.ShapeDtypeStruct((B, S, D), F32),
        grid=(S // L,),
        in_specs=[row_spec, _const_spec((1, D)), _const_spec((D, A_Z_DIM + GATE_LANES)),
                  _const_spec((GATE_ROWS, D)),
                  _const_spec((1, GATE_LANES)), _const_spec((GATE_ROWS, 1)),
                  _const_spec((1, A_NV)), _const_spec((A_NV, D))],
        out_specs=row_spec,
        scratch_shapes=[pltpu.VMEM((B * A_HEADS, A_QK_DIM, A_V_DIM), F32),
                        pltpu.VMEM((B * A_HEADS, 1, A_QK_DIM), F32),
                        pltpu.VMEM((B * A_HEADS, 1, 1), F32),
                        pltpu.VMEM((B, L, A_NV), BF16)],
        compiler_params=pltpu.CompilerParams(
            dimension_semantics=("arbitrary",), vmem_limit_bytes=VMEM_LIMIT),
        name="mlstm_layer",
    )(x, norm_g[None, :], wz, wgr, bc, br, hnorm_g.reshape(1, A_NV), w_out.astype(BF16))


def _ffn_kernel(x_ref, g_ref, wup_ref, cw_ref, cb_ref, wdn_ref, fg_ref, *rest, final_norm):
    if final_norm:
        out_ref, ubuf_ref, act_ref, slab_ref = rest
    else:
        out_ref, xh0_ref, xh1_ref, xh2_ref, ubuf_ref, act_ref, slab_ref = rest
    T = x_ref.shape[0]
    H = CONV_HALO
    half = T // 2
    nslab = D_MODEL // LANES

    @pl.when(pl.program_id(1) == 0)
    def _():
        ubuf_ref[:, 0:H, :] = jnp.zeros((2 * D_FF // LANES, H, LANES), F32)

    x = x_ref[...]
    xn = (x * _rms_scale(x) * g_ref[...]).astype(BF16)
    per = FFN_COLS // LANES

    def up(c):
        for base in (0, D_FF):
            u = _dot(xn, wup_ref[:, base + c * FFN_COLS:base + (c + 1) * FFN_COLS])
            for s in range(per):
                ubuf_ref[base // LANES + c * per + s, H:H + T, :] = u[:, s * LANES:(s + 1) * LANES]

    def conv(k, parity):
        cs = slice(k * LANES, (k + 1) * LANES)
        taps = [ubuf_ref[k, pl.ds(H + parity - 2 + i, half, stride=2), :] for i in range(3)]
        return (taps[0] * cw_ref[0:1, cs] + taps[1] * cw_ref[1:2, cs] + taps[2] * cw_ref[2:3, cs]
                + cb_ref[:, cs])

    def activate(c):
        for j in range(c * per, (c + 1) * per):
            for parity in range(2):
                gate = conv(j, parity)
                val = conv(D_FF // LANES + j, parity)
                act_ref[parity * half:(parity + 1) * half, j * LANES:(j + 1) * LANES] = (
                    gate * jax.nn.sigmoid(gate) * val).astype(BF16)

    nchunk = D_FF // FFN_COLS
    up(0)
    for c in range(nchunk):
        if c + 1 < nchunk:
            up(c + 1)
        activate(c)

    ubuf_ref[:, 0:H, :] = ubuf_ref[:, T:T + H, :]
    down = _dot(act_ref[...], wdn_ref[...])
    for k in range(nslab):
        for parity in range(2):
            slab_ref[k, pl.ds(parity, half, stride=2), :] = (
                down[parity * half:(parity + 1) * half, k * LANES:(k + 1) * LANES])
    y = x + jnp.concatenate([slab_ref[k] for k in range(nslab)], axis=1)
    if final_norm:
        out_ref[...] = y * _rms_scale(y) * fg_ref[...]
        return
    out_ref[...] = y
    xh = y * _rms_scale(y)
    xh0_ref[...] = xh.astype(BF16)
    for k in range(nslab):
        slab_ref[k] = xh[:, k * LANES:(k + 1) * LANES]
    for dil, ref in ((B_GROUPS[1][1], xh1_ref), (B_GROUPS[2][1], xh2_ref)):
        for r in range(dil):
            for k in range(nslab):
                c0 = r * D_MODEL + k * LANES
                ref[:, c0:c0 + LANES] = slab_ref[k, pl.ds(r, T // dil, stride=dil), :].astype(BF16)


def _conv_ffn(x, norm_g, w_up, conv_w, conv_b, w_down, final_g, final_norm):
    B, S, D = x.shape
    T = FFN_ROWS
    row_spec = pl.BlockSpec((None, T, D), lambda b, t: (b, t, 0))
    out_shape = [jax.ShapeDtypeStruct((B, S, D), F32)]
    out_specs = [row_spec]
    scratch = [pltpu.VMEM((2 * D_FF // LANES, T + CONV_HALO, LANES), F32),
               pltpu.VMEM((T, D_FF), BF16), pltpu.VMEM((D // LANES, T, LANES), F32)]
    if not final_norm:
        out_shape.append(jax.ShapeDtypeStruct((B, S, D), BF16))
        out_specs.append(row_spec)
        for _, dil in B_GROUPS[1:]:
            out_shape.append(jax.ShapeDtypeStruct((B, S // dil, dil * D), BF16))
            out_specs.append(pl.BlockSpec((None, T // dil, dil * D), lambda b, t: (b, t, 0)))
    res = pl.pallas_call(
        functools.partial(_ffn_kernel, final_norm=final_norm),
        out_shape=out_shape,
        grid=(B, S // T),
        in_specs=[row_spec, _const_spec((1, D)), _const_spec((D, 2 * D_FF)),
                  _const_spec((3, 2 * D_FF)), _const_spec((1, 2 * D_FF)),
                  _const_spec((D_FF, D)), _const_spec((1, D))],
        out_specs=out_specs,
        scratch_shapes=scratch,
        compiler_params=pltpu.CompilerParams(
            dimension_semantics=("arbitrary", "arbitrary"), vmem_limit_bytes=VMEM_LIMIT),
        name="conv_ffn_final" if final_norm else "conv_ffn",
    )(x, norm_g[None, :], w_up.astype(BF16), conv_w, conv_b[None, :], w_down.astype(BF16),
      final_g[None, :])
    return res[0] if final_norm else res


def _attn_kernel(xh_ref, wq_ref, wkt_ref, wv_ref, bias_ref, o_ref, lse_ref,
                 q2_ref, kt_ref, v_ref):
    T = xh_ref.shape[0]
    P = B_BLOCK
    i = pl.program_id(2)

    @pl.when(i == 0)
    def _():
        kt_ref[:, 0:P] = jnp.zeros((D_MODEL, P), BF16)
        v_ref[0:P, :] = jnp.zeros((P, D_MODEL), BF16)

    @pl.when(i > 0)
    def _():
        kt_ref[:, 0:P] = kt_ref[:, T:T + P]
        v_ref[0:P, :] = v_ref[T:T + P, :]

    lane = lax.broadcasted_iota(jnp.int32, (P, D_MODEL), 1)
    even_head = (lane % (2 * B_HEAD_DIM)) < B_HEAD_DIM
    lane2 = lax.broadcasted_iota(jnp.int32, (P, 2 * B_HEAD_DIM), 1)
    first_half = lane2 < B_HEAD_DIM
    ones = jnp.ones((2 * P, LANES), BF16)
    first = (i == 0).astype(jnp.int32)

    xh = xh_ref[...]
    q = _dot(xh, wq_ref[...]) * (B_HEAD_DIM ** -0.5)
    for j in range(T // P):
        qj = q[j * P:(j + 1) * P]
        q2_ref[j, 0:P, :] = jnp.where(even_head, qj, 0.0).astype(BF16)
        q2_ref[j, P:2 * P, :] = jnp.where(even_head, 0.0, qj).astype(BF16)
    kt_ref[:, P:P + T] = _dot_nt(wkt_ref[...], xh).astype(BF16)
    v_ref[P:P + T, :] = _dot(xh, wv_ref[...]).astype(BF16)

    for j in range(T // P):
        lse_all = jnp.zeros((P, LANES), F32)
        for hp in range(B_HEADS // 2):
            cs = slice(hp * 2 * B_HEAD_DIM, (hp + 1) * 2 * B_HEAD_DIM)
            s = (_dot(q2_ref[j, :, cs], kt_ref[cs, j * P:(j + 2) * P])
                 + bias_ref[first if j == 0 else 0, hp])
            m = jnp.max(s, axis=-1, keepdims=True)
            p = jnp.exp(s - m).astype(BF16)
            va = jnp.concatenate([v_ref[j * P:(j + 2) * P, cs], ones], axis=1)
            pv = _dot(p, va)
            l = pv[:, LANES:]
            o = pv[:, :LANES] * (1.0 / l)
            lse = m + jnp.log(l)
            o_ref[j * P:(j + 1) * P, cs] = jnp.where(first_half, o[0:P], o[P:2 * P]).astype(BF16)
            lse_all = jnp.where(lane2 == 2 * hp, lse[0:P],
                                jnp.where(lane2 == 2 * hp + 1, lse[P:2 * P], lse_all))
        lse_ref[j * P:(j + 1) * P, :] = lse_all


def _t5_bucket(dist):
    max_exact = REL_BUCKETS // 2
    d = np.maximum(dist, 0)
    log_ratio = np.log(np.maximum(d, 1) / max_exact) / math.log(REL_MAX_DIST / max_exact)
    large = np.minimum(max_exact + (log_ratio * (REL_BUCKETS - max_exact)).astype(np.int64),
                       REL_BUCKETS - 1)
    return np.where(d < max_exact, d, large).astype(np.int32)


def _band_bias(rel_bias, g, win, dil):
    P = B_BLOCK
    n = 3 * P
    t = np.arange(n)
    delta = P - np.where(t < 2 * P, t, t - n)
    valid = (delta >= 0) & (delta <= win // dil)
    vec = rel_bias[_t5_bucket(delta * dil)][:, g * B_HEADS:(g + 1) * B_HEADS].astype(F32)
    vec = jnp.where(valid[:, None], vec, MASK_VALUE).T
    bias = jnp.tile(vec, (1, P))[:, :P * (n - 1)].reshape(B_HEADS, P, n - 1)[:, :, :2 * P]
    bias = bias.reshape(B_HEADS // 2, 2 * P, 2 * P)
    no_prev = jnp.where(np.arange(2 * P) < P, MASK_VALUE, bias)
    return jnp.stack([bias, no_prev])


def _attn_group(xh, g, win, dil, wq_all, wkt_all, wkv_all, rel_bias):
    B, n, _ = xh.shape
    D = D_MODEL
    T = ATT_ROWS
    cls_spec = pl.BlockSpec((None, T, D), lambda b, r, i: (b, i, r))
    lse_spec = pl.BlockSpec((None, T, LANES), lambda b, r, i: (b, i, r))

    def weight_spec(row_blk, col_blk):
        return pl.BlockSpec((D, D), lambda *_: (row_blk, col_blk), pipeline_mode=pl.Buffered(1))

    return pl.pallas_call(
        _attn_kernel,
        out_shape=[jax.ShapeDtypeStruct((B, n, dil * D), BF16),
                   jax.ShapeDtypeStruct((B, n, dil * LANES), F32)],
        grid=(B, dil, n // T),
        in_specs=[cls_spec, weight_spec(0, g), weight_spec(g, 0), weight_spec(0, len(B_GROUPS) + g),
                  _const_spec((2, B_HEADS // 2, 2 * B_BLOCK, 2 * B_BLOCK))],
        out_specs=[cls_spec, lse_spec],
        scratch_shapes=[pltpu.VMEM((T // B_BLOCK, 2 * B_BLOCK, D), BF16),
                        pltpu.VMEM((D, T + B_BLOCK), BF16), pltpu.VMEM((T + B_BLOCK, D), BF16)],
        compiler_params=pltpu.CompilerParams(
            dimension_semantics=("arbitrary", "arbitrary", "arbitrary"),
            vmem_limit_bytes=VMEM_LIMIT),
        name=f"dilated_attn_g{g}",
    )(xh, wq_all, wkt_all, wkv_all, _band_bias(rel_bias, g, win, dil))


def _merge_kernel(x_ref, o0_ref, o1_ref, o2_ref, l0_ref, l1_ref, l2_ref, ex_ref, wo_ref, out_ref,
                  os1_ref, os2_ref, ls1_ref, ls2_ref, mg_ref):
    T = x_ref.shape[0]
    nslab = D_MODEL // LANES
    for dil, o_src, o_dst, l_src, l_dst in ((B_GROUPS[1][1], o1_ref, os1_ref, l1_ref, ls1_ref),
                                            (B_GROUPS[2][1], o2_ref, os2_ref, l2_ref, ls2_ref)):
        for r in range(dil):
            rows = pl.ds(r, T // dil, stride=dil)
            l_dst[rows, :] = l_src[:, r * LANES:(r + 1) * LANES]
            for k in range(nslab):
                c0 = r * D_MODEL + k * LANES
                o_dst[k, rows, :] = o_src[:, c0:c0 + LANES].astype(F32)
    l0, l1, l2 = l0_ref[...], ls1_ref[...], ls2_ref[...]
    m = jnp.maximum(jnp.maximum(l0, l1), l2)
    e0, e1, e2 = jnp.exp(l0 - m), jnp.exp(l1 - m), jnp.exp(l2 - m)
    inv = 1.0 / (e0 + e1 + e2)

    def spread(w):
        return _dot(w.astype(BF16), ex_ref[...])

    w0, w1 = spread(e0 * inv), spread(e1 * inv)
    for k in range(nslab):
        ks = slice(k * LANES, (k + 1) * LANES)
        o2 = os2_ref[k]
        mg = o2 + w0[:, ks] * (o0_ref[:, ks].astype(F32) - o2) + w1[:, ks] * (os1_ref[k] - o2)
        mg_ref[:, ks] = mg.astype(BF16)
    out_ref[...] = x_ref[...] + _dot(mg_ref[...], wo_ref[...])


def _merge_groups(x, outs, lses, w_out):
    B, S, D = x.shape
    T = MERGE_ROWS
    row_spec = pl.BlockSpec((None, T, D), lambda b, t: (b, t, 0))
    o_specs = [pl.BlockSpec((None, T // dil, dil * D), lambda b, t: (b, t, 0)) for _, dil in B_GROUPS]
    l_specs = [pl.BlockSpec((None, T // dil, dil * LANES), lambda b, t: (b, t, 0))
               for _, dil in B_GROUPS]
    expand = np.zeros((LANES, D), np.float32)
    for h in range(B_HEADS):
        expand[h, h * B_HEAD_DIM:(h + 1) * B_HEAD_DIM] = 1.0
    nslab = D // LANES
    return pl.pallas_call(
        _merge_kernel,
        out_shape=jax.ShapeDtypeStruct((B, S, D), F32),
        grid=(B, S // T),
        in_specs=[row_spec] + o_specs + l_specs + [_const_spec((LANES, D)), _const_spec((D, D))],
        out_specs=row_spec,
        scratch_shapes=[pltpu.VMEM((nslab, T, LANES), F32), pltpu.VMEM((nslab, T, LANES), F32),
                        pltpu.VMEM((T, LANES), F32), pltpu.VMEM((T, LANES), F32),
                        pltpu.VMEM((T, D), BF16)],
        compiler_params=pltpu.CompilerParams(
            dimension_semantics=("arbitrary", "arbitrary"), vmem_limit_bytes=VMEM_LIMIT),
        name="merge_groups",
    )(x, *outs, *lses, jnp.asarray(expand, BF16), w_out.astype(BF16))


def kernel(x, a_norm_g, a_w_in, a_b_if, a_hnorm_g, a_w_out, kv_norm_g, w_kv, b_norm_g, b_w_q,
           b_w_out, rel_bias, f_norm_g, f_w_up, f_conv_w, f_conv_b, f_w_down, final_norm_g):
    x = _mlstm_layer(x, a_norm_g[0], a_w_in[0], a_b_if[0], a_hnorm_g[0], a_w_out[0])
    x, *streams = _conv_ffn(x, f_norm_g[0], f_w_up[0], f_conv_w[0], f_conv_b[0], f_w_down[0],
                            final_norm_g, final_norm=False)
    att_w = len(B_GROUPS) * B_HEADS * B_HEAD_DIM
    wq_all = (b_norm_g[0][:, None] * b_w_q[0]).astype(BF16)
    wkv_all = (kv_norm_g[:, None] * w_kv).astype(BF16)
    wkt_all = wkv_all[:, :att_w].T
    outs, lses = [], []
    for g, (win, dil) in enumerate(B_GROUPS):
        o, lse = _attn_group(streams[g], g, win, dil, wq_all, wkt_all, wkv_all, rel_bias)
        outs.append(o)
        lses.append(lse)
    x = _merge_groups(x, outs, lses, b_w_out[0])
    return _conv_ffn(x, f_norm_g[1], f_w_up[1], f_conv_w[1], f_conv_b[1], f_w_down[1],
                     final_norm_g, final_norm=True)
```

```python
import functools
import math

import numpy as np
import jax
import jax.numpy as jnp
from jax import lax
from jax.experimental import pallas as pl
from jax.experimental.pallas import tpu as pltpu

F32 = jnp.float32
BF16 = jnp.bfloat16

D_MODEL = 1024
A_HEADS = 4
A_QK_DIM = 128
A_V_DIM = 256
A_NQ = A_HEADS * A_QK_DIM
A_NV = A_HEADS * A_V_DIM
A_Z_DIM = 2 * A_NQ + 2 * A_NV
GATE_SOFTCAP = 15.0
B_GROUPS = ((128, 1), (512, 4), (2048, 16))
B_HEAD_DIM = 64
B_HEADS = 16
B_BLOCK = 128
REL_BUCKETS = 32
REL_MAX_DIST = 2048
D_FF = 2816
EPS = 1e-6
MASK_VALUE = -1e30

LANES = 128
SUBLANES = 8
BF16_ROWS = 2 * SUBLANES
MXU_TILE = 256
V7X_VMEM_BYTES = 64 * 1024 * 1024

GATE_LANES = LANES
GATE_ROWS = BF16_ROWS
MLSTM_CHUNK = 256
FFN_ROWS = 512
FFN_COLS = MXU_TILE
CONV_HALO = SUBLANES
ATT_ROWS = 1024
MERGE_ROWS = 1024
VMEM_LIMIT = V7X_VMEM_BYTES * 7 // 8


def _rms_scale(x):
    return lax.rsqrt(jnp.mean(x * x, axis=-1, keepdims=True) + EPS)


def _softcap(z):
    return GATE_SOFTCAP * jnp.tanh(z / GATE_SOFTCAP)


def _log_sigmoid(a):
    return jnp.minimum(a, 0.0) - jnp.log1p(jnp.exp(-jnp.abs(a)))


def _split3(v):
    hi = v.astype(BF16)
    r1 = v - hi.astype(F32)
    mid = r1.astype(BF16)
    lo = (r1 - mid.astype(F32)).astype(BF16)
    return hi, mid, lo


def _dot(a, b):
    return jnp.dot(a, b, preferred_element_type=F32)


def _dot_nt(a, b):
    return lax.dot_general(a, b, (((1,), (1,)), ((), ())), preferred_element_type=F32)


def _dot_tn(a, b):
    return lax.dot_general(a, b, (((0,), (0,)), ((), ())), preferred_element_type=F32)


def _const_spec(shape):
    return pl.BlockSpec(shape, lambda *_: (0,) * len(shape), pipeline_mode=pl.Buffered(1))


def _mlstm_kernel(x_ref, g_ref, wz_ref, wgr_ref, bc_ref, br_ref, hg_ref, wo_ref,
                  out_ref, c_ref, n_ref, m_ref, hcat_ref):
    NB, L = x_ref.shape[0], x_ref.shape[1]

    @pl.when(pl.program_id(0) == 0)
    def _():
        c_ref[...] = jnp.zeros_like(c_ref)
        n_ref[...] = jnp.zeros_like(n_ref)
        m_ref[...] = jnp.zeros_like(m_ref)

    row = lax.broadcasted_iota(jnp.int32, (L, L), 0)
    col = lax.broadcasted_iota(jnp.int32, (L, L), 1)
    causal = col <= row
    tril = jnp.where(causal, 1.0, 0.0).astype(BF16)
    triu = jnp.where(row <= col, 1.0, 0.0).astype(BF16)

    def projection(b):
        x = x_ref[b]
        xn = (x * _rms_scale(x) * g_ref[...]).astype(BF16)
        z = {}

        def gates():
            z["ac"] = _softcap(_dot(xn, wz_ref[:, A_Z_DIM:]) + bc_ref[...])
            z["ar"] = _softcap(_dot_nt(wgr_ref[...], xn) + br_ref[...])
            z["bcs"] = sum(_dot(tril, p) for p in _split3(_log_sigmoid(z["ac"])))
            z["brs"] = sum(_dot(p, triu) for p in _split3(_log_sigmoid(z["ar"])))

        def columns(name, c0, c1):
            z[name] = _dot(xn, wz_ref[:, c0:c1])

        return z, [gates,
                   functools.partial(columns, "qk", 0, 2 * A_NQ),
                   functools.partial(columns, "v", 2 * A_NQ, 2 * A_NQ + A_NV),
                   functools.partial(columns, "o", 2 * A_NQ + A_NV, A_Z_DIM)]

    def head(h, b, z):
        st = b * A_HEADS + h
        q = z["qk"][:, h * A_QK_DIM:(h + 1) * A_QK_DIM] * (A_QK_DIM ** -0.5)
        k = z["qk"][:, A_NQ + h * A_QK_DIM:A_NQ + (h + 1) * A_QK_DIM]
        v = z["v"][:, h * A_V_DIM:(h + 1) * A_V_DIM]
        o = z["o"][:, h * A_V_DIM:(h + 1) * A_V_DIM]
        qb, kb, vb = q.astype(BF16), k.astype(BF16), v.astype(BF16)
        li_c = z["ac"][:, h:h + 1]
        li_r = z["ar"][h:h + 1, :]
        b_c = z["bcs"][:, A_HEADS + h:A_HEADS + h + 1]
        b_r = z["brs"][A_HEADS + h:A_HEADS + h + 1, :]
        m_prev = m_ref[st]
        c_prev = c_ref[st]
        n_prev = n_ref[st]

        log_d = jnp.where(causal, b_c - b_r + li_r, -jnp.inf)
        m_inter = b_c + m_prev
        m_t = jnp.maximum(m_inter, jnp.max(log_d, axis=-1, keepdims=True))
        sm = _dot_nt(qb, kb) * jnp.exp(log_d - m_t)
        w_inter = jnp.exp(m_inter - m_t)
        num = _dot(sm.astype(BF16), vb) + w_inter * _dot(qb, c_prev.astype(BF16))
        den = (jnp.sum(sm, axis=-1, keepdims=True)
               + w_inter * jnp.sum(q * n_prev, axis=-1, keepdims=True))
        hv = num * (1.0 / jnp.maximum(jnp.abs(den), jnp.exp(-m_t)))
        hv = hv * _rms_scale(hv) * hg_ref[:, h * A_V_DIM:(h + 1) * A_V_DIM]
        hcat_ref[b, :, h * A_V_DIM:(h + 1) * A_V_DIM] = (hv * jax.nn.sigmoid(o)).astype(BF16)

        b_last = b_c[L - 1:L, :]
        g_c = b_last - b_c + li_c
        g_r = b_last - b_r + li_r
        m_new = jnp.maximum(b_last + m_prev, jnp.max(g_r, axis=-1, keepdims=True))
        decay = jnp.exp(b_last + m_prev - m_new)
        kw = k * jnp.exp(g_c - m_new)
        c_ref[st] = decay * c_prev + _dot_tn(kw.astype(BF16), vb)
        n_ref[st] = decay * n_prev + jnp.sum(kw, axis=0, keepdims=True)
        m_ref[st] = m_new

    z_prev = None
    for b in range(NB):
        z, pieces = projection(b)
        for h in range(A_HEADS):
            if z_prev is not None:
                head(h, b - 1, z_prev)
            pieces[h]()
        z_prev = z
    for h in range(A_HEADS):
        head(h, NB - 1, z_prev)

    for b in range(NB):
        out_ref[b] = x_ref[b] + _dot(hcat_ref[b], wo_ref[...])


def _mlstm_layer(x, norm_g, w_in, b_if, hnorm_g, w_out):
    B, S, D = x.shape
    L = MLSTM_CHUNK
    wg = w_in[:, A_Z_DIM:]
    ng = 2 * A_HEADS
    wz = jnp.pad(w_in, ((0, 0), (0, GATE_LANES - ng))).astype(BF16)
    wgr = jnp.pad(wg.T, ((0, GATE_ROWS - ng), (0, 0))).astype(BF16)
    bc = jnp.pad(b_if[None, :], ((0, 0), (0, GATE_LANES - ng)))
    br = jnp.pad(b_if[:, None], ((0, GATE_ROWS - ng), (0, 0)))
    row_spec = pl.BlockSpec((B, L, D), lambda c: (0, c, 0))
    return pl.pallas_call(
        _mlstm_kernel,
        out_shape=jax.ShapeDtypeStruct((B, S, D), F32),
        grid=(S // L,),
        in_specs=[row_spec, _const_spec((1, D)), _const_spec((D, A_Z_DIM + GATE_LANES)),
                  _const_spec((GATE_ROWS, D)),
                  _const_spec((1, GATE_LANES)), _const_spec((GATE_ROWS, 1)),
                  _const_spec((1, A_NV)), _const_spec((A_NV, D))],
        out_specs=row_spec,
        scratch_shapes=[pltpu.VMEM((B * A_HEADS, A_QK_DIM, A_V_DIM), F32),
                        pltpu.VMEM((B * A_HEADS, 1, A_QK_DIM), F32),
                        pltpu.VMEM((B * A_HEADS, 1, 1), F32),
                        pltpu.VMEM((B, L, A_NV), BF16)],
        compiler_params=pltpu.CompilerParams(
            dimension_semantics=("arbitrary",), vmem_limit_bytes=VMEM_LIMIT),
        name="mlstm_layer",
    )(x, norm_g[None, :], wz, wgr, bc, br, hnorm_g.reshape(1, A_NV), w_out.astype(BF16))


def _ffn_kernel(x_ref, *rest, final_layer):
    if final_layer:
        (branch_ref, g_ref, wup_ref, cw_ref, cb_ref, wdn_ref, fg_ref,
         out_ref, ubuf_ref, act_ref, slab_ref) = rest
    else:
        (g_ref, wup_ref, cw_ref, cb_ref, wdn_ref,
         out_ref, xh0_ref, xh1_ref, xh2_ref, ubuf_ref, act_ref, slab_ref, cls_ref) = rest
    T = x_ref.shape[0]
    H = CONV_HALO
    half = T // 2
    nslab = D_MODEL // LANES

    @pl.when(pl.program_id(1) == 0)
    def _():
        ubuf_ref[:, 0:H, :] = jnp.zeros((2 * D_FF // LANES, H, LANES), F32)

    x = x_ref[...]
    if final_layer:
        x = x + branch_ref[...].astype(F32)
    xn = (x * _rms_scale(x) * g_ref[...]).astype(BF16)
    per = FFN_COLS // LANES

    def up(c):
        for base in (0, D_FF):
            u = _dot(xn, wup_ref[:, base + c * FFN_COLS:base + (c + 1) * FFN_COLS])
            for s in range(per):
                ubuf_ref[base // LANES + c * per + s, H:H + T, :] = u[:, s * LANES:(s + 1) * LANES]

    def conv(k, parity):
        cs = slice(k * LANES, (k + 1) * LANES)
        taps = [ubuf_ref[k, pl.ds(H + parity - 2 + i, half, stride=2), :] for i in range(3)]
        return (taps[0] * cw_ref[0:1, cs] + taps[1] * cw_ref[1:2, cs] + taps[2] * cw_ref[2:3, cs]
                + cb_ref[:, cs])

    def activate(c):
        for j in range(c * per, (c + 1) * per):
            for parity in range(2):
                gate = conv(j, parity)
                val = conv(D_FF // LANES + j, parity)
                act_ref[parity * half:(parity + 1) * half, j * LANES:(j + 1) * LANES] = (
                    gate * jax.nn.sigmoid(gate) * val).astype(BF16)

    nchunk = D_FF // FFN_COLS
    up(0)
    for c in range(nchunk):
        if c + 1 < nchunk:
            up(c + 1)
        activate(c)

    ubuf_ref[:, 0:H, :] = ubuf_ref[:, T:T + H, :]
    down = _dot(act_ref[...], wdn_ref[...])
    for k in range(nslab):
        for parity in range(2):
            slab_ref[k, pl.ds(parity, half, stride=2), :] = (
                down[parity * half:(parity + 1) * half, k * LANES:(k + 1) * LANES])
    y = x + jnp.concatenate([slab_ref[k] for k in range(nslab)], axis=1)
    if final_layer:
        out_ref[...] = y * _rms_scale(y) * fg_ref[...]
        return
    out_ref[...] = y
    xh = y * _rms_scale(y)
    xh0_ref[...] = xh.astype(BF16)
    for k in range(nslab):
        slab_ref[k] = xh[:, k * LANES:(k + 1) * LANES]
    d1, d2 = B_GROUPS[1][1], B_GROUPS[2][1]
    sub = d2 // d1
    for r1 in range(d1):
        for k in range(nslab):
            rows = slab_ref[k, pl.ds(r1, T // d1, stride=d1), :]
            cls_ref[r1 * nslab + k] = rows
            c0 = r1 * D_MODEL + k * LANES
            xh1_ref[:, c0:c0 + LANES] = rows.astype(BF16)
    for r1 in range(d1):
        for q in range(sub):
            for k in range(nslab):
                c0 = (r1 + d1 * q) * D_MODEL + k * LANES
                xh2_ref[:, c0:c0 + LANES] = (
                    cls_ref[r1 * nslab + k, pl.ds(q, T // d2, stride=sub), :].astype(BF16))


def _conv_ffn(x, norm_g, w_up, conv_w, conv_b, w_down, branch=None, final_g=None):
    B, S, D = x.shape
    T = FFN_ROWS
    final_layer = branch is not None
    row_spec = pl.BlockSpec((None, T, D), lambda b, t: (b, t, 0))
    weight_specs = [_const_spec((1, D)), _const_spec((D, 2 * D_FF)), _const_spec((3, 2 * D_FF)),
                    _const_spec((1, 2 * D_FF)), _const_spec((D_FF, D))]
    weights = [norm_g[None, :], w_up.astype(BF16), conv_w, conv_b[None, :], w_down.astype(BF16)]
    out_shape = [jax.ShapeDtypeStruct((B, S, D), F32)]
    out_specs = [row_spec]
    scratch = [pltpu.VMEM((2 * D_FF // LANES, T + CONV_HALO, LANES), F32),
               pltpu.VMEM((T, D_FF), BF16), pltpu.VMEM((D // LANES, T, LANES), F32)]
    if final_layer:
        in_specs = [row_spec, row_spec] + weight_specs + [_const_spec((1, D))]
        operands = [x, branch] + weights + [final_g[None, :]]
    else:
        in_specs = [row_spec] + weight_specs
        operands = [x] + weights
        out_shape.append(jax.ShapeDtypeStruct((B, S, D), BF16))
        out_specs.append(row_spec)
        for _, dil in B_GROUPS[1:]:
            out_shape.append(jax.ShapeDtypeStruct((B, S // dil, dil * D), BF16))
            out_specs.append(pl.BlockSpec((None, T // dil, dil * D), lambda b, t: (b, t, 0)))
        d1 = B_GROUPS[1][1]
        scratch.append(pltpu.VMEM((d1 * D // LANES, T // d1, LANES), F32))
    res = pl.pallas_call(
        functools.partial(_ffn_kernel, final_layer=final_layer),
        out_shape=out_shape,
        grid=(B, S // T),
        in_specs=in_specs,
        out_specs=out_specs,
        scratch_shapes=scratch,
        compiler_params=pltpu.CompilerParams(
            dimension_semantics=("arbitrary", "arbitrary"), vmem_limit_bytes=VMEM_LIMIT),
        name="conv_ffn_final" if final_layer else "conv_ffn",
    )(*operands)
    return res[0] if final_layer else res


def _attn_kernel(xh_ref, wq_ref, wkt_ref, wv_ref, bias_ref, o_ref, lse_ref,
                 q2_ref, kt_ref, v_ref):
    T = xh_ref.shape[0]
    P = B_BLOCK
    i = pl.program_id(2)

    @pl.when(i == 0)
    def _():
        kt_ref[:, 0:P] = jnp.zeros((D_MODEL, P), BF16)
        v_ref[0:P, :] = jnp.zeros((P, D_MODEL), BF16)

    @pl.when(i > 0)
    def _():
        kt_ref[:, 0:P] = kt_ref[:, T:T + P]
        v_ref[0:P, :] = v_ref[T:T + P, :]

    lane = lax.broadcasted_iota(jnp.int32, (P, D_MODEL), 1)
    even_head = (lane % (2 * B_HEAD_DIM)) < B_HEAD_DIM
    lane2 = lax.broadcasted_iota(jnp.int32, (P, 2 * B_HEAD_DIM), 1)
    first_half = lane2 < B_HEAD_DIM
    ones = jnp.ones((2 * P, LANES), BF16)
    first = (i == 0).astype(jnp.int32)

    xh = xh_ref[...]
    q = _dot(xh, wq_ref[...]) * (B_HEAD_DIM ** -0.5)
    for j in range(T // P):
        qj = q[j * P:(j + 1) * P]
        q2_ref[j, 0:P, :] = jnp.where(even_head, qj, 0.0).astype(BF16)
        q2_ref[j, P:2 * P, :] = jnp.where(even_head, 0.0, qj).astype(BF16)
    kt_ref[:, P:P + T] = _dot_nt(wkt_ref[...], xh).astype(BF16)
    v_ref[P:P + T, :] = _dot(xh, wv_ref[...]).astype(BF16)

    for j in range(T // P):
        lse_all = jnp.zeros((P, LANES), F32)
        for hp in range(B_HEADS // 2):
            cs = slice(hp * 2 * B_HEAD_DIM, (hp + 1) * 2 * B_HEAD_DIM)
            s = (_dot(q2_ref[j, :, cs], kt_ref[cs, j * P:(j + 2) * P])
                 + bias_ref[first if j == 0 else 0, hp])
            m = jnp.max(s, axis=-1, keepdims=True)
            p = jnp.exp(s - m).astype(BF16)
            va = jnp.concatenate([v_ref[j * P:(j + 2) * P, cs], ones], axis=1)
            pv = _dot(p, va)
            l = pv[:, LANES:]
            o = pv[:, :LANES] * (1.0 / l)
            lse = m + jnp.log(l)
            o_ref[j * P:(j + 1) * P, cs] = jnp.where(first_half, o[0:P], o[P:2 * P]).astype(BF16)
            lse_all = jnp.where(lane2 == 2 * hp, lse[0:P],
                                jnp.where(lane2 == 2 * hp + 1, lse[P:2 * P], lse_all))
        lse_ref[j * P:(j + 1) * P, :] = lse_all


def _t5_bucket(dist):
    max_exact = REL_BUCKETS // 2
    d = np.maximum(dist, 0)
    log_ratio = np.log(np.maximum(d, 1) / max_exact) / math.log(REL_MAX_DIST / max_exact)
    large = np.minimum(max_exact + (log_ratio * (REL_BUCKETS - max_exact)).astype(np.int64),
                       REL_BUCKETS - 1)
    return np.where(d < max_exact, d, large).astype(np.int32)


def _band_bias(rel_bias, g, win, dil):
    P = B_BLOCK
    n = 3 * P
    t = np.arange(n)
    delta = P - np.where(t < 2 * P, t, t - n)
    valid = (delta >= 0) & (delta <= win // dil)
    vec = rel_bias[_t5_bucket(delta * dil)][:, g * B_HEADS:(g + 1) * B_HEADS].astype(F32)
    vec = jnp.where(valid[:, None], vec, MASK_VALUE).T
    bias = jnp.tile(vec, (1, P))[:, :P * (n - 1)].reshape(B_HEADS, P, n - 1)[:, :, :2 * P]
    bias = bias.reshape(B_HEADS // 2, 2 * P, 2 * P)
    no_prev = jnp.where(np.arange(2 * P) < P, MASK_VALUE, bias)
    return jnp.stack([bias, no_prev])


def _attn_group(xh, g, win, dil, wq_all, wkt_all, wkv_all, rel_bias):
    B, n, _ = xh.shape
    D = D_MODEL
    T = ATT_ROWS
    cls_spec = pl.BlockSpec((None, T, D), lambda b, r, i: (b, i, r))
    lse_spec = pl.BlockSpec((None, T, LANES), lambda b, r, i: (b, i, r))

    def weight_spec(row_blk, col_blk):
        return pl.BlockSpec((D, D), lambda *_: (row_blk, col_blk), pipeline_mode=pl.Buffered(1))

    return pl.pallas_call(
        _attn_kernel,
        out_shape=[jax.ShapeDtypeStruct((B, n, dil * D), BF16),
                   jax.ShapeDtypeStruct((B, n, dil * LANES), F32)],
        grid=(B, dil, n // T),
        in_specs=[cls_spec, weight_spec(0, g), weight_spec(g, 0), weight_spec(0, len(B_GROUPS) + g),
                  _const_spec((2, B_HEADS // 2, 2 * B_BLOCK, 2 * B_BLOCK))],
        out_specs=[cls_spec, lse_spec],
        scratch_shapes=[pltpu.VMEM((T // B_BLOCK, 2 * B_BLOCK, D), BF16),
                        pltpu.VMEM((D, T + B_BLOCK), BF16), pltpu.VMEM((T + B_BLOCK, D), BF16)],
        compiler_params=pltpu.CompilerParams(
            dimension_semantics=("arbitrary", "arbitrary", "arbitrary"),
            vmem_limit_bytes=VMEM_LIMIT),
        name=f"dilated_attn_g{g}",
    )(xh, wq_all, wkt_all, wkv_all, _band_bias(rel_bias, g, win, dil))


def _merge_kernel(o0_ref, o1_ref, o2_ref, l0_ref, l1_ref, l2_ref, ex_ref, wo_ref, out_ref,
                  os1_ref, os2_ref, ls1_ref, ls2_ref, mg_ref):
    T = o0_ref.shape[0]
    nslab = D_MODEL // LANES
    for dil, o_src, o_dst, l_src, l_dst in ((B_GROUPS[1][1], o1_ref, os1_ref, l1_ref, ls1_ref),
                                            (B_GROUPS[2][1], o2_ref, os2_ref, l2_ref, ls2_ref)):
        for r in range(dil):
            rows = pl.ds(r, T // dil, stride=dil)
            l_dst[rows, :] = l_src[:, r * LANES:(r + 1) * LANES]
            for k in range(nslab):
                c0 = r * D_MODEL + k * LANES
                o_dst[k, rows, :] = o_src[:, c0:c0 + LANES].astype(F32)
    l0, l1, l2 = l0_ref[...], ls1_ref[...], ls2_ref[...]
    m = jnp.maximum(jnp.maximum(l0, l1), l2)
    e0, e1, e2 = jnp.exp(l0 - m), jnp.exp(l1 - m), jnp.exp(l2 - m)
    inv = 1.0 / (e0 + e1 + e2)

    def spread(w):
        return _dot(w.astype(BF16), ex_ref[...])

    w0, w1 = spread(e0 * inv), spread(e1 * inv)
    for k in range(nslab):
        ks = slice(k * LANES, (k + 1) * LANES)
        o2 = os2_ref[k]
        mg = o2 + w0[:, ks] * (o0_ref[:, ks].astype(F32) - o2) + w1[:, ks] * (os1_ref[k] - o2)
        mg_ref[:, ks] = mg.astype(BF16)
    out_ref[...] = _dot(mg_ref[...], wo_ref[...]).astype(BF16)


def _merge_groups(outs, lses, w_out):
    B, S, D = outs[0].shape
    T = MERGE_ROWS
    row_spec = pl.BlockSpec((None, T, D), lambda b, t: (b, t, 0))
    o_specs = [pl.BlockSpec((None, T // dil, dil * D), lambda b, t: (b, t, 0)) for _, dil in B_GROUPS]
    l_specs = [pl.BlockSpec((None, T // dil, dil * LANES), lambda b, t: (b, t, 0))
               for _, dil in B_GROUPS]
    expand = np.zeros((LANES, D), np.float32)
    for h in range(B_HEADS):
        expand[h, h * B_HEAD_DIM:(h + 1) * B_HEAD_DIM] = 1.0
    nslab = D // LANES
    return pl.pallas_call(
        _merge_kernel,
        out_shape=jax.ShapeDtypeStruct((B, S, D), BF16),
        grid=(B, S // T),
        in_specs=o_specs + l_specs + [_const_spec((LANES, D)), _const_spec((D, D))],
        out_specs=row_spec,
        scratch_shapes=[pltpu.VMEM((nslab, T, LANES), F32), pltpu.VMEM((nslab, T, LANES), F32),
                        pltpu.VMEM((T, LANES), F32), pltpu.VMEM((T, LANES), F32),
                        pltpu.VMEM((T, D), BF16)],
        compiler_params=pltpu.CompilerParams(
            dimension_semantics=("arbitrary", "arbitrary"), vmem_limit_bytes=VMEM_LIMIT),
        name="merge_groups",
    )(*outs, *lses, jnp.asarray(expand, BF16), w_out.astype(BF16))


def kernel(x, a_norm_g, a_w_in, a_b_if, a_hnorm_g, a_w_out, kv_norm_g, w_kv, b_norm_g, b_w_q,
           b_w_out, rel_bias, f_norm_g, f_w_up, f_conv_w, f_conv_b, f_w_down, final_norm_g):
    x = _mlstm_layer(x, a_norm_g[0], a_w_in[0], a_b_if[0], a_hnorm_g[0], a_w_out[0])
    x, *streams = _conv_ffn(x, f_norm_g[0], f_w_up[0], f_conv_w[0], f_conv_b[0], f_w_down[0])
    att_w = len(B_GROUPS) * B_HEADS * B_HEAD_DIM
    wq_all = (b_norm_g[0][:, None] * b_w_q[0]).astype(BF16)
    wkv_all = (kv_norm_g[:, None] * w_kv).astype(BF16)
    wkt_all = wkv_all[:, :att_w].T
    outs, lses = [], []
    for g, (win, dil) in enumerate(B_GROUPS):
        o, lse = _attn_group(streams[g], g, win, dil, wq_all, wkt_all, wkv_all, rel_bias)
        outs.append(o)
        lses.append(lse)
    branch = _merge_groups(outs, lses, b_w_out[0])
    return _conv_ffn(x, f_norm_g[1], f_w_up[1], f_conv_w[1], f_conv_b[1], f_w_down[1],
                     branch=branch, final_g=final_norm_g)
```

```python
import functools
import math

import numpy as np
import jax
import jax.numpy as jnp
from jax import lax
from jax.experimental import pallas as pl
from jax.experimental.pallas import tpu as pltpu

F32 = jnp.float32
BF16 = jnp.bfloat16

D_MODEL = 1024
A_HEADS = 4
A_QK_DIM = 128
A_V_DIM = 256
A_NQ = A_HEADS * A_QK_DIM
A_NV = A_HEADS * A_V_DIM
A_Z_DIM = 2 * A_NQ + 2 * A_NV
GATE_SOFTCAP = 15.0
B_GROUPS = ((128, 1), (512, 4), (2048, 16))
B_HEAD_DIM = 64
B_HEADS = 16
B_BLOCK = 128
REL_BUCKETS = 32
REL_MAX_DIST = 2048
D_FF = 2816
EPS = 1e-6
MASK_VALUE = -1e30

LANES = 128
SUBLANES = 8
BF16_ROWS = 2 * SUBLANES
MXU_TILE = 256
V7X_VMEM_BYTES = 64 * 1024 * 1024

GATE_LANES = LANES
GATE_ROWS = BF16_ROWS
MLSTM_CHUNK = 256
FFN_ROWS = 512
FFN_COLS = MXU_TILE
CONV_HALO = SUBLANES
ATT_ROWS = 1024
MERGE_ROWS = 1024
VMEM_LIMIT = V7X_VMEM_BYTES * 7 // 8


def _rms_scale(x):
    return lax.rsqrt(jnp.mean(x * x, axis=-1, keepdims=True) + EPS)


def _softcap(z):
    return GATE_SOFTCAP * jnp.tanh(z / GATE_SOFTCAP)


def _log_sigmoid(a):
    return jnp.minimum(a, 0.0) - jnp.log1p(jnp.exp(-jnp.abs(a)))


def _split3(v):
    hi = v.astype(BF16)
    r1 = v - hi.astype(F32)
    mid = r1.astype(BF16)
    lo = (r1 - mid.astype(F32)).astype(BF16)
    return hi, mid, lo


def _dot(a, b):
    return jnp.dot(a, b, preferred_element_type=F32)


def _dot_nt(a, b):
    return lax.dot_general(a, b, (((1,), (1,)), ((), ())), preferred_element_type=F32)


def _dot_tn(a, b):
    return lax.dot_general(a, b, (((0,), (0,)), ((), ())), preferred_element_type=F32)


def _const_spec(shape):
    return pl.BlockSpec(shape, lambda *_: (0,) * len(shape), pipeline_mode=pl.Buffered(1))


def _mlstm_kernel(x_ref, g_ref, wz_ref, wgr_ref, bc_ref, br_ref, hg_ref, wo_ref,
                  out_ref, c_ref, n_ref, m_ref, hcat_ref):
    NB, L = x_ref.shape[0], x_ref.shape[1]

    @pl.when(pl.program_id(0) == 0)
    def _():
        c_ref[...] = jnp.zeros_like(c_ref)
        n_ref[...] = jnp.zeros_like(n_ref)
        m_ref[...] = jnp.zeros_like(m_ref)

    row = lax.broadcasted_iota(jnp.int32, (L, L), 0)
    col = lax.broadcasted_iota(jnp.int32, (L, L), 1)
    causal = col <= row
    tril = jnp.where(causal, 1.0, 0.0).astype(BF16)
    triu = jnp.where(row <= col, 1.0, 0.0).astype(BF16)

    def projection(b):
        x = x_ref[b]
        xn = (x * _rms_scale(x) * g_ref[...]).astype(BF16)
        z = {}

        def gates():
            z["ac"] = _softcap(_dot(xn, wz_ref[:, A_Z_DIM:]) + bc_ref[...])
            z["ar"] = _softcap(_dot_nt(wgr_ref[...], xn) + br_ref[...])
            z["bcs"] = sum(_dot(tril, p) for p in _split3(_log_sigmoid(z["ac"])))
            z["brs"] = sum(_dot(p, triu) for p in _split3(_log_sigmoid(z["ar"])))

        def columns(name, c0, c1):
            z[name] = _dot(xn, wz_ref[:, c0:c1])

        return z, [gates,
                   functools.partial(columns, "qk", 0, 2 * A_NQ),
                   functools.partial(columns, "v", 2 * A_NQ, 2 * A_NQ + A_NV),
                   functools.partial(columns, "o", 2 * A_NQ + A_NV, A_Z_DIM)]

    def head(h, b, z):
        st = b * A_HEADS + h
        q = z["qk"][:, h * A_QK_DIM:(h + 1) * A_QK_DIM] * (A_QK_DIM ** -0.5)
        k = z["qk"][:, A_NQ + h * A_QK_DIM:A_NQ + (h + 1) * A_QK_DIM]
        v = z["v"][:, h * A_V_DIM:(h + 1) * A_V_DIM]
        o = z["o"][:, h * A_V_DIM:(h + 1) * A_V_DIM]
        qb, kb, vb = q.astype(BF16), k.astype(BF16), v.astype(BF16)
        li_c = z["ac"][:, h:h + 1]
        li_r = z["ar"][h:h + 1, :]
        b_c = z["bcs"][:, A_HEADS + h:A_HEADS + h + 1]
        b_r = z["brs"][A_HEADS + h:A_HEADS + h + 1, :]
        m_prev = m_ref[st]
        c_prev = c_ref[st]
        n_prev = n_ref[st]

        log_d = jnp.where(causal, b_c - b_r + li_r, -jnp.inf)
        m_inter = b_c + m_prev
        m_t = jnp.maximum(m_inter, jnp.max(log_d, axis=-1, keepdims=True))
        sm = _dot_nt(qb, kb) * jnp.exp(log_d - m_t)
        w_inter = jnp.exp(m_inter - m_t)
        num = _dot(sm.astype(BF16), vb) + w_inter * _dot(qb, c_prev.astype(BF16))
        den = (jnp.sum(sm, axis=-1, keepdims=True)
               + w_inter * jnp.sum(q * n_prev, axis=-1, keepdims=True))
        hv = num * (1.0 / jnp.maximum(jnp.abs(den), jnp.exp(-m_t)))
        hv = hv * _rms_scale(hv) * hg_ref[:, h * A_V_DIM:(h + 1) * A_V_DIM]
        hcat_ref[b, :, h * A_V_DIM:(h + 1) * A_V_DIM] = (hv * jax.nn.sigmoid(o)).astype(BF16)

        b_last = b_c[L - 1:L, :]
        g_c = b_last - b_c + li_c
        g_r = b_last - b_r + li_r
        m_new = jnp.maximum(b_last + m_prev, jnp.max(g_r, axis=-1, keepdims=True))
        decay = jnp.exp(b_last + m_prev - m_new)
        kw = k * jnp.exp(g_c - m_new)
        c_ref[st] = decay * c_prev + _dot_tn(kw.astype(BF16), vb)
        n_ref[st] = decay * n_prev + jnp.sum(kw, axis=0, keepdims=True)
        m_ref[st] = m_new

    z_prev = None
    for b in range(NB):
        z, pieces = projection(b)
        for h in range(A_HEADS):
            if z_prev is not None:
                head(h, b - 1, z_prev)
            pieces[h]()
        z_prev = z
    for h in range(A_HEADS):
        head(h, NB - 1, z_prev)

    for b in range(NB):
        out_ref[b] = x_ref[b] + _dot(hcat_ref[b], wo_ref[...])


def _mlstm_layer(x, norm_g, w_in, b_if, hnorm_g, w_out):
    B, S, D = x.shape
    L = MLSTM_CHUNK
    wg = w_in[:, A_Z_DIM:]
    ng = 2 * A_HEADS
    wz = jnp.pad(w_in, ((0, 0), (0, GATE_LANES - ng))).astype(BF16)
    wgr = jnp.pad(wg.T, ((0, GATE_ROWS - ng), (0, 0))).astype(BF16)
    bc = jnp.pad(b_if[None, :], ((0, 0), (0, GATE_LANES - ng)))
    br = jnp.pad(b_if[:, None], ((0, GATE_ROWS - ng), (0, 0)))
    row_spec = pl.BlockSpec((B, L, D), lambda c: (0, c, 0))
    return pl.pallas_call(
        _mlstm_kernel,
        out_shape=jax.ShapeDtypeStruct((B, S, D), F32),
        grid=(S // L,),
        in_specs=[row_spec, _const_spec((1, D)), _const_spec((D, A_Z_DIM + GATE_LANES)),
                  _const_spec((GATE_ROWS, D)),
                  _const_spec((1, GATE_LANES)), _const_spec((GATE_ROWS, 1)),
                  _const_spec((1, A_NV)), _const_spec((A_NV, D))],
        out_specs=row_spec,
        scratch_shapes=[pltpu.VMEM((B * A_HEADS, A_QK_DIM, A_V_DIM), F32),
                        pltpu.VMEM((B * A_HEADS, 1, A_QK_DIM), F32),
                        pltpu.VMEM((B * A_HEADS, 1, 1), F32),
                        pltpu.VMEM((B, L, A_NV), BF16)],
        compiler_params=pltpu.CompilerParams(
            dimension_semantics=("arbitrary",), vmem_limit_bytes=VMEM_LIMIT),
        name="mlstm_layer",
    )(x, norm_g[None, :], wz, wgr, bc, br, hnorm_g.reshape(1, A_NV), w_out.astype(BF16))


def _ffn_kernel(x_ref, *rest, final_layer):
    if final_layer:
        (branch_ref, g_ref, wup_ref, cw_ref, cb_ref, wdn_ref, fg_ref,
         out_ref, ubuf_ref, act_ref, slab_ref) = rest
    else:
        (g_ref, wup_ref, cw_ref, cb_ref, wdn_ref,
         out_ref, xh0_ref, xh1_ref, xh2_ref, ubuf_ref, act_ref, slab_ref, cls_ref) = rest
    T = x_ref.shape[0]
    H = CONV_HALO
    half = T // 2
    nslab = D_MODEL // LANES

    @pl.when(pl.program_id(1) == 0)
    def _():
        ubuf_ref[:, 0:H, :] = jnp.zeros((2 * D_FF // LANES, H, LANES), F32)

    x = x_ref[...]
    if final_layer:
        x = x + branch_ref[...].astype(F32)
    xn = (x * _rms_scale(x) * g_ref[...]).astype(BF16)
    per = FFN_COLS // LANES

    def up(c):
        for base in (0, D_FF):
            u = _dot(xn, wup_ref[:, base + c * FFN_COLS:base + (c + 1) * FFN_COLS])
            for s in range(per):
                ubuf_ref[base // LANES + c * per + s, H:H + T, :] = u[:, s * LANES:(s + 1) * LANES]

    def conv(k, parity):
        cs = slice(k * LANES, (k + 1) * LANES)
        taps = [ubuf_ref[k, pl.ds(H + parity - 2 + i, half, stride=2), :] for i in range(3)]
        return (taps[0] * cw_ref[0:1, cs] + taps[1] * cw_ref[1:2, cs] + taps[2] * cw_ref[2:3, cs]
                + cb_ref[:, cs])

    def activate(c):
        for j in range(c * per, (c + 1) * per):
            for parity in range(2):
                gate = conv(j, parity)
                val = conv(D_FF // LANES + j, parity)
                act_ref[parity * half:(parity + 1) * half, j * LANES:(j + 1) * LANES] = (
                    gate * jax.nn.sigmoid(gate) * val).astype(BF16)

    nchunk = D_FF // FFN_COLS
    up(0)
    for c in range(nchunk):
        if c + 1 < nchunk:
            up(c + 1)
        activate(c)

    ubuf_ref[:, 0:H, :] = ubuf_ref[:, T:T + H, :]
    down = _dot(act_ref[...], wdn_ref[...])
    for k in range(nslab):
        for parity in range(2):
            slab_ref[k, pl.ds(parity, half, stride=2), :] = (
                down[parity * half:(parity + 1) * half, k * LANES:(k + 1) * LANES])
    y = x + jnp.concatenate([slab_ref[k] for k in range(nslab)], axis=1)
    if final_layer:
        out_ref[...] = y * _rms_scale(y) * fg_ref[...]
        return
    out_ref[...] = y
    xh = y * _rms_scale(y)
    xh0_ref[...] = xh.astype(BF16)
    for k in range(nslab):
        slab_ref[k] = xh[:, k * LANES:(k + 1) * LANES]
    d1, d2 = B_GROUPS[1][1], B_GROUPS[2][1]
    sub = d2 // d1
    for r1 in range(d1):
        for k in range(nslab):
            rows = slab_ref[k, pl.ds(r1, T // d1, stride=d1), :]
            cls_ref[r1 * nslab + k] = rows
            c0 = r1 * D_MODEL + k * LANES
            xh1_ref[:, c0:c0 + LANES] = rows.astype(BF16)
    for r1 in range(d1):
        for q in range(sub):
            for k in range(nslab):
                c0 = (r1 + d1 * q) * D_MODEL + k * LANES
                xh2_ref[:, c0:c0 + LANES] = (
                    cls_ref[r1 * nslab + k, pl.ds(q, T // d2, stride=sub), :].astype(BF16))


def _conv_ffn(x, norm_g, w_up, conv_w, conv_b, w_down, branch=None, final_g=None):
    B, S, D = x.shape
    T = FFN_ROWS
    final_layer = branch is not None
    row_spec = pl.BlockSpec((None, T, D), lambda b, t: (b, t, 0))
    weight_specs = [_const_spec((1, D)), _const_spec((D, 2 * D_FF)), _const_spec((3, 2 * D_FF)),
                    _const_spec((1, 2 * D_FF)), _const_spec((D_FF, D))]
    weights = [norm_g[None, :], w_up.astype(BF16), conv_w, conv_b[None, :], w_down.astype(BF16)]
    out_shape = [jax.ShapeDtypeStruct((B, S, D), F32)]
    out_specs = [row_spec]
    scratch = [pltpu.VMEM((2 * D_FF // LANES, T + CONV_HALO, LANES), F32),
               pltpu.VMEM((T, D_FF), BF16), pltpu.VMEM((D // LANES, T, LANES), F32)]
    if final_layer:
        in_specs = [row_spec, row_spec] + weight_specs + [_const_spec((1, D))]
        operands = [x, branch] + weights + [final_g[None, :]]
    else:
        in_specs = [row_spec] + weight_specs
        operands = [x] + weights
        out_shape.append(jax.ShapeDtypeStruct((B, S, D), BF16))
        out_specs.append(row_spec)
        for _, dil in B_GROUPS[1:]:
            out_shape.append(jax.ShapeDtypeStruct((B, S // dil, dil * D), BF16))
            out_specs.append(pl.BlockSpec((None, T // dil, dil * D), lambda b, t: (b, t, 0)))
        d1 = B_GROUPS[1][1]
        scratch.append(pltpu.VMEM((d1 * D // LANES, T // d1, LANES), F32))
    res = pl.pallas_call(
        functools.partial(_ffn_kernel, final_layer=final_layer),
        out_shape=out_shape,
        grid=(B, S // T),
        in_specs=in_specs,
        out_specs=out_specs,
        scratch_shapes=scratch,
        compiler_params=pltpu.CompilerParams(
            dimension_semantics=("arbitrary", "arbitrary"), vmem_limit_bytes=VMEM_LIMIT),
        name="conv_ffn_final" if final_layer else "conv_ffn",
    )(*operands)
    return res[0] if final_layer else res


def _attn_kernel(xh_ref, wq_ref, wkt_ref, wv_ref, band_ref, o_ref, lse_ref,
                 q2_ref, kt_ref, v_ref, bias_ref):
    T = xh_ref.shape[0]
    P = B_BLOCK
    i = pl.program_id(2)

    @pl.when((pl.program_id(0) == 0) & (pl.program_id(1) == 0) & (i == 0))
    def _():
        prev_key = lax.broadcasted_iota(jnp.int32, (2 * P, 2 * P), 1) < P
        for hp in range(B_HEADS // 2):
            bias_ref[0, hp] = band_ref[hp]
            bias_ref[1, hp] = jnp.where(prev_key, MASK_VALUE, band_ref[hp])

    @pl.when(i == 0)
    def _():
        kt_ref[:, 0:P] = jnp.zeros((D_MODEL, P), BF16)
        v_ref[0:P, :] = jnp.zeros((P, D_MODEL), BF16)

    @pl.when(i > 0)
    def _():
        kt_ref[:, 0:P] = kt_ref[:, T:T + P]
        v_ref[0:P, :] = v_ref[T:T + P, :]

    lane = lax.broadcasted_iota(jnp.int32, (P, D_MODEL), 1)
    even_head = (lane % (2 * B_HEAD_DIM)) < B_HEAD_DIM
    lane2 = lax.broadcasted_iota(jnp.int32, (P, 2 * B_HEAD_DIM), 1)
    first_half = lane2 < B_HEAD_DIM
    ones = jnp.ones((2 * P, LANES), BF16)
    first = (i == 0).astype(jnp.int32)

    xh = xh_ref[...]
    q = _dot(xh, wq_ref[...]) * (B_HEAD_DIM ** -0.5)
    for j in range(T // P):
        qj = q[j * P:(j + 1) * P]
        q2_ref[j, 0:P, :] = jnp.where(even_head, qj, 0.0).astype(BF16)
        q2_ref[j, P:2 * P, :] = jnp.where(even_head, 0.0, qj).astype(BF16)
    kt_ref[:, P:P + T] = _dot_nt(wkt_ref[...], xh).astype(BF16)
    v_ref[P:P + T, :] = _dot(xh, wv_ref[...]).astype(BF16)

    for j in range(T // P):
        lse_all = jnp.zeros((P, LANES), F32)
        for hp in range(B_HEADS // 2):
            cs = slice(hp * 2 * B_HEAD_DIM, (hp + 1) * 2 * B_HEAD_DIM)
            s = (_dot(q2_ref[j, :, cs], kt_ref[cs, j * P:(j + 2) * P])
                 + bias_ref[first if j == 0 else 0, hp])
            m = jnp.max(s, axis=-1, keepdims=True)
            p = jnp.exp(s - m).astype(BF16)
            va = jnp.concatenate([v_ref[j * P:(j + 2) * P, cs], ones], axis=1)
            pv = _dot(p, va)
            l = pv[:, LANES:]
            o = pv[:, :LANES] * (1.0 / l)
            lse = m + jnp.log(l)
            o_ref[j * P:(j + 1) * P, cs] = jnp.where(first_half, o[0:P], o[P:2 * P]).astype(BF16)
            lse_all = jnp.where(lane2 == 2 * hp, lse[0:P],
                                jnp.where(lane2 == 2 * hp + 1, lse[P:2 * P], lse_all))
        lse_ref[j * P:(j + 1) * P, :] = lse_all


def _t5_bucket(dist):
    max_exact = REL_BUCKETS // 2
    d = np.maximum(dist, 0)
    log_ratio = np.log(np.maximum(d, 1) / max_exact) / math.log(REL_MAX_DIST / max_exact)
    large = np.minimum(max_exact + (log_ratio * (REL_BUCKETS - max_exact)).astype(np.int64),
                       REL_BUCKETS - 1)
    return np.where(d < max_exact, d, large).astype(np.int32)


def _band_bias(rel_bias):
    P = B_BLOCK
    n = 3 * P
    t = np.arange(n)
    delta = P - np.where(t < 2 * P, t, t - n)
    vecs = []
    for g, (win, dil) in enumerate(B_GROUPS):
        valid = (delta >= 0) & (delta <= win // dil)
        vec = rel_bias[_t5_bucket(delta * dil)][:, g * B_HEADS:(g + 1) * B_HEADS].astype(F32)
        vecs.append(jnp.where(valid[:, None], vec, MASK_VALUE).T)
    vec = jnp.concatenate(vecs)
    bias = jnp.tile(vec, (1, P))[:, :P * (n - 1)].reshape(len(vecs) * B_HEADS, P, n - 1)
    return bias[:, :, :2 * P].reshape(len(vecs), B_HEADS // 2, 2 * P, 2 * P)


def _attn_group(xh, g, dil, wq_all, wkt_all, wkv_all, bias_all):
    B, n, _ = xh.shape
    D = D_MODEL
    T = ATT_ROWS
    cls_spec = pl.BlockSpec((None, T, D), lambda b, r, i: (b, i, r))
    lse_spec = pl.BlockSpec((None, T, LANES), lambda b, r, i: (b, i, r))

    def weight_spec(row_blk, col_blk):
        return pl.BlockSpec((D, D), lambda *_: (row_blk, col_blk), pipeline_mode=pl.Buffered(1))

    return pl.pallas_call(
        _attn_kernel,
        out_shape=[jax.ShapeDtypeStruct((B, n, dil * D), BF16),
                   jax.ShapeDtypeStruct((B, n, dil * LANES), F32)],
        grid=(B, dil, n // T),
        in_specs=[cls_spec, weight_spec(0, g), weight_spec(g, 0), weight_spec(0, len(B_GROUPS) + g),
                  pl.BlockSpec((None,) + bias_all.shape[1:], lambda *_: (g, 0, 0, 0),
                               pipeline_mode=pl.Buffered(1))],
        out_specs=[cls_spec, lse_spec],
        scratch_shapes=[pltpu.VMEM((T // B_BLOCK, 2 * B_BLOCK, D), BF16),
                        pltpu.VMEM((D, T + B_BLOCK), BF16), pltpu.VMEM((T + B_BLOCK, D), BF16),
                        pltpu.VMEM((2,) + bias_all.shape[1:], F32)],
        compiler_params=pltpu.CompilerParams(
            dimension_semantics=("arbitrary", "arbitrary", "arbitrary"),
            vmem_limit_bytes=VMEM_LIMIT),
        name=f"dilated_attn_g{g}",
    )(xh, wq_all, wkt_all, wkv_all, bias_all)


def _merge_kernel(o0_ref, o1_ref, o2_ref, l0_ref, l1_ref, l2_ref, ex_ref, wo_ref, out_ref,
                  os1_ref, os2_ref, ls1_ref, ls2_ref, mg_ref):
    T = o0_ref.shape[0]
    nslab = D_MODEL // LANES
    dilated = ((B_GROUPS[1][1], o1_ref, os1_ref, l1_ref, ls1_ref),
               (B_GROUPS[2][1], o2_ref, os2_ref, l2_ref, ls2_ref))
    for dil, _, _, l_src, l_dst in dilated:
        for r in range(dil):
            l_dst[pl.ds(r, T // dil, stride=dil), :] = l_src[:, r * LANES:(r + 1) * LANES]
    l0, l1, l2 = l0_ref[...], ls1_ref[...], ls2_ref[...]
    m = jnp.maximum(jnp.maximum(l0, l1), l2)
    e0, e1, e2 = jnp.exp(l0 - m), jnp.exp(l1 - m), jnp.exp(l2 - m)
    inv = 1.0 / (e0 + e1 + e2)

    def spread(w):
        return _dot(w.astype(BF16), ex_ref[...])

    w0, w1 = spread(e0 * inv), spread(e1 * inv)

    for k in range(nslab):
        for dil, o_src, o_dst, _, _ in dilated:
            for r in range(dil):
                c0 = r * D_MODEL + k * LANES
                o_dst[k, pl.ds(r, T // dil, stride=dil), :] = o_src[:, c0:c0 + LANES].astype(F32)
        ks = slice(k * LANES, (k + 1) * LANES)
        o2 = os2_ref[k]
        mg = o2 + w0[:, ks] * (o0_ref[:, ks].astype(F32) - o2) + w1[:, ks] * (os1_ref[k] - o2)
        mg_ref[:, ks] = mg.astype(BF16)
    out_ref[...] = _dot(mg_ref[...], wo_ref[...]).astype(BF16)


def _merge_groups(outs, lses, w_out):
    B, S, D = outs[0].shape
    T = MERGE_ROWS
    row_spec = pl.BlockSpec((None, T, D), lambda b, t: (b, t, 0))
    o_specs = [pl.BlockSpec((None, T // dil, dil * D), lambda b, t: (b, t, 0)) for _, dil in B_GROUPS]
    l_specs = [pl.BlockSpec((None, T // dil, dil * LANES), lambda b, t: (b, t, 0))
               for _, dil in B_GROUPS]
    expand = np.zeros((LANES, D), np.float32)
    for h in range(B_HEADS):
        expand[h, h * B_HEAD_DIM:(h + 1) * B_HEAD_DIM] = 1.0
    nslab = D // LANES
    return pl.pallas_call(
        _merge_kernel,
        out_shape=jax.ShapeDtypeStruct((B, S, D), BF16),
        grid=(B, S // T),
        in_specs=o_specs + l_specs + [_const_spec((LANES, D)), _const_spec((D, D))],
        out_specs=row_spec,
        scratch_shapes=[pltpu.VMEM((nslab, T, LANES), F32), pltpu.VMEM((nslab, T, LANES), F32),
                        pltpu.VMEM((T, LANES), F32), pltpu.VMEM((T, LANES), F32),
                        pltpu.VMEM((T, D), BF16)],
        compiler_params=pltpu.CompilerParams(
            dimension_semantics=("arbitrary", "arbitrary"), vmem_limit_bytes=VMEM_LIMIT),
        name="merge_groups",
    )(*outs, *lses, jnp.asarray(expand, BF16), w_out.astype(BF16))


def kernel(x, a_norm_g, a_w_in, a_b_if, a_hnorm_g, a_w_out, kv_norm_g, w_kv, b_norm_g, b_w_q,
           b_w_out, rel_bias, f_norm_g, f_w_up, f_conv_w, f_conv_b, f_w_down, final_norm_g):
    x = _mlstm_layer(x, a_norm_g[0], a_w_in[0], a_b_if[0], a_hnorm_g[0], a_w_out[0])
    x, *streams = _conv_ffn(x, f_norm_g[0], f_w_up[0], f_conv_w[0], f_conv_b[0], f_w_down[0])
    att_w = len(B_GROUPS) * B_HEADS * B_HEAD_DIM
    wq_all = (b_norm_g[0][:, None] * b_w_q[0]).astype(BF16)
    wkv_all = (kv_norm_g[:, None] * w_kv).astype(BF16)
    wkt_all = wkv_all[:, :att_w].T
    bias_all = _band_bias(rel_bias)
    outs, lses = [], []
    for g, (_, dil) in enumerate(B_GROUPS):
        o, lse = _attn_group(streams[g], g, dil, wq_all, wkt_all, wkv_all, bias_all)
        outs.append(o)
        lses.append(lse)
    branch = _merge_groups(outs, lses, b_w_out[0])
    return _conv_ffn(x, f_norm_g[1], f_w_up[1], f_conv_w[1], f_conv_b[1], f_w_down[1],
                     branch=branch, final_g=final_norm_g)
```

```python
import functools
import math

import numpy as np
import jax
import jax.numpy as jnp
from jax import lax
from jax.experimental import pallas as pl
from jax.experimental.pallas import tpu as pltpu

F32 = jnp.float32
BF16 = jnp.bfloat16

D_MODEL = 1024
A_HEADS = 4
A_QK_DIM = 128
A_V_DIM = 256
A_NQ = A_HEADS * A_QK_DIM
A_NV = A_HEADS * A_V_DIM
A_Z_DIM = 2 * A_NQ + 2 * A_NV
GATE_SOFTCAP = 15.0
B_GROUPS = ((128, 1), (512, 4), (2048, 16))
B_HEAD_DIM = 64
B_HEADS = 16
B_BLOCK = 128
REL_BUCKETS = 32
REL_MAX_DIST = 2048
D_FF = 2816
EPS = 1e-6
MASK_VALUE = -1e30

LANES = 128
SUBLANES = 8
BF16_ROWS = 2 * SUBLANES
MXU_TILE = 256
V7X_VMEM_BYTES = 64 * 1024 * 1024

GATE_LANES = LANES
GATE_ROWS = BF16_ROWS
MLSTM_CHUNK = 256
FFN_ROWS = 512
FFN_COLS = MXU_TILE
CONV_HALO = SUBLANES
ATT_ROWS = 1024
MERGE_ROWS = 1024
VMEM_LIMIT = V7X_VMEM_BYTES * 7 // 8


def _rms_scale(x):
    return lax.rsqrt(jnp.mean(x * x, axis=-1, keepdims=True) + EPS)


def _softcap(z):
    return GATE_SOFTCAP * jnp.tanh(z / GATE_SOFTCAP)


def _log_sigmoid(a):
    return jnp.minimum(a, 0.0) - jnp.log1p(jnp.exp(-jnp.abs(a)))


def _split3(v):
    hi = v.astype(BF16)
    r1 = v - hi.astype(F32)
    mid = r1.astype(BF16)
    lo = (r1 - mid.astype(F32)).astype(BF16)
    return hi, mid, lo


def _dot(a, b):
    return jnp.dot(a, b, preferred_element_type=F32)


def _dot_nt(a, b):
    return lax.dot_general(a, b, (((1,), (1,)), ((), ())), preferred_element_type=F32)


def _dot_tn(a, b):
    return lax.dot_general(a, b, (((0,), (0,)), ((), ())), preferred_element_type=F32)


def _const_spec(shape):
    return pl.BlockSpec(shape, lambda *_: (0,) * len(shape), pipeline_mode=pl.Buffered(1))


def _mlstm_kernel(x_ref, g_ref, wz_ref, wgr_ref, bc_ref, br_ref, hg_ref, wo_ref,
                  out_ref, c_ref, n_ref, m_ref, hcat_ref):
    NB, L = x_ref.shape[0], x_ref.shape[1]

    @pl.when(pl.program_id(0) == 0)
    def _():
        c_ref[...] = jnp.zeros_like(c_ref)
        n_ref[...] = jnp.zeros_like(n_ref)
        m_ref[...] = jnp.zeros_like(m_ref)

    row = lax.broadcasted_iota(jnp.int32, (L, L), 0)
    col = lax.broadcasted_iota(jnp.int32, (L, L), 1)
    causal = col <= row
    tril = jnp.where(causal, 1.0, 0.0).astype(BF16)
    triu = jnp.where(row <= col, 1.0, 0.0).astype(BF16)

    def projection(b):
        x = x_ref[b]
        xn = (x * _rms_scale(x) * g_ref[...]).astype(BF16)
        z = {}

        def gates():
            z["ac"] = _softcap(_dot(xn, wz_ref[:, A_Z_DIM:]) + bc_ref[...])
            z["ar"] = _softcap(_dot_nt(wgr_ref[...], xn) + br_ref[...])
            z["bcs"] = sum(_dot(tril, p) for p in _split3(_log_sigmoid(z["ac"])))
            z["brs"] = sum(_dot(p, triu) for p in _split3(_log_sigmoid(z["ar"])))

        def columns(name, c0, c1):
            z[name] = _dot(xn, wz_ref[:, c0:c1])

        return z, [gates,
                   functools.partial(columns, "qk", 0, 2 * A_NQ),
                   functools.partial(columns, "v", 2 * A_NQ, 2 * A_NQ + A_NV),
                   functools.partial(columns, "o", 2 * A_NQ + A_NV, A_Z_DIM)]

    def head(h, b, z):
        st = b * A_HEADS + h
        q = z["qk"][:, h * A_QK_DIM:(h + 1) * A_QK_DIM] * (A_QK_DIM ** -0.5)
        k = z["qk"][:, A_NQ + h * A_QK_DIM:A_NQ + (h + 1) * A_QK_DIM]
        v = z["v"][:, h * A_V_DIM:(h + 1) * A_V_DIM]
        o = z["o"][:, h * A_V_DIM:(h + 1) * A_V_DIM]
        qb, kb, vb = q.astype(BF16), k.astype(BF16), v.astype(BF16)
        li_c = z["ac"][:, h:h + 1]
        li_r = z["ar"][h:h + 1, :]
        b_c = z["bcs"][:, A_HEADS + h:A_HEADS + h + 1]
        b_r = z["brs"][A_HEADS + h:A_HEADS + h + 1, :]
        m_prev = m_ref[st]
        c_prev = c_ref[st]
        n_prev = n_ref[st]

        log_d = jnp.where(causal, b_c - b_r + li_r, -jnp.inf)
        m_inter = b_c + m_prev
        m_t = jnp.maximum(m_inter, jnp.max(log_d, axis=-1, keepdims=True))
        sm = _dot_nt(qb, kb) * jnp.exp(log_d - m_t)
        w_inter = jnp.exp(m_inter - m_t)
        num = _dot(sm.astype(BF16), vb) + w_inter * _dot(qb, c_prev.astype(BF16))
        den = (jnp.sum(sm, axis=-1, keepdims=True)
               + w_inter * jnp.sum(q * n_prev, axis=-1, keepdims=True))
        hv = num * (1.0 / jnp.maximum(jnp.abs(den), jnp.exp(-m_t)))
        hv = hv * _rms_scale(hv) * hg_ref[:, h * A_V_DIM:(h + 1) * A_V_DIM]
        hcat_ref[b, :, h * A_V_DIM:(h + 1) * A_V_DIM] = (hv * jax.nn.sigmoid(o)).astype(BF16)

        b_last = b_c[L - 1:L, :]
        g_c = b_last - b_c + li_c
        g_r = b_last - b_r + li_r
        m_new = jnp.maximum(b_last + m_prev, jnp.max(g_r, axis=-1, keepdims=True))
        decay = jnp.exp(b_last + m_prev - m_new)
        kw = k * jnp.exp(g_c - m_new)
        c_ref[st] = decay * c_prev + _dot_tn(kw.astype(BF16), vb)
        n_ref[st] = decay * n_prev + jnp.sum(kw, axis=0, keepdims=True)
        m_ref[st] = m_new

    z_prev = None
    for b in range(NB):
        z, pieces = projection(b)
        for h in range(A_HEADS):
            if z_prev is not None:
                head(h, b - 1, z_prev)
            pieces[h]()
        z_prev = z
    for h in range(A_HEADS):
        head(h, NB - 1, z_prev)

    for b in range(NB):
        out_ref[b] = x_ref[b] + _dot(hcat_ref[b], wo_ref[...])


def _mlstm_layer(x, norm_g, w_in, b_if, hnorm_g, w_out):
    B, S, D = x.shape
    L = MLSTM_CHUNK
    wg = w_in[:, A_Z_DIM:]
    ng = 2 * A_HEADS
    wz = jnp.pad(w_in, ((0, 0), (0, GATE_LANES - ng))).astype(BF16)
    wgr = jnp.pad(wg.T, ((0, GATE_ROWS - ng), (0, 0))).astype(BF16)
    bc = jnp.pad(b_if[None, :], ((0, 0), (0, GATE_LANES - ng)))
    br = jnp.pad(b_if[:, None], ((0, GATE_ROWS - ng), (0, 0)))
    row_spec = pl.BlockSpec((B, L, D), lambda c: (0, c, 0))
    return pl.pallas_call(
        _mlstm_kernel,
        out_shape=jax.ShapeDtypeStruct((B, S, D), F32),
        grid=(S // L,),
        in_specs=[row_spec, _const_spec((1, D)), _const_spec((D, A_Z_DIM + GATE_LANES)),
                  _const_spec((GATE_ROWS, D)),
                  _const_spec((1, GATE_LANES)), _const_spec((GATE_ROWS, 1)),
                  _const_spec((1, A_NV)), _const_spec((A_NV, D))],
        out_specs=row_spec,
        scratch_shapes=[pltpu.VMEM((B * A_HEADS, A_QK_DIM, A_V_DIM), F32),
                        pltpu.VMEM((B * A_HEADS, 1, A_QK_DIM), F32),
                        pltpu.VMEM((B * A_HEADS, 1, 1), F32),
                        pltpu.VMEM((B, L, A_NV), BF16)],
        compiler_params=pltpu.CompilerParams(
            dimension_semantics=("arbitrary",), vmem_limit_bytes=VMEM_LIMIT),
        name="mlstm_layer",
    )(x, norm_g[None, :], wz, wgr, bc, br, hnorm_g.reshape(1, A_NV), w_out.astype(BF16))


def _ffn_kernel(x_ref, *rest, final_layer):
    if final_layer:
        (branch_ref, g_ref, wup_ref, cw_ref, cb_ref, wdn_ref, fg_ref,
         out_ref, ubuf_ref, act_ref, slab_ref) = rest
    else:
        (g_ref, wup_ref, cw_ref, cb_ref, wdn_ref,
         out_ref, xh0_ref, xh1_ref, xh2_ref, ubuf_ref, act_ref, slab_ref, cls_ref) = rest
    T = x_ref.shape[0]
    H = CONV_HALO
    half = T // 2
    nslab = D_MODEL // LANES

    @pl.when(pl.program_id(1) == 0)
    def _():
        ubuf_ref[:, 0:H, :] = jnp.zeros((2 * D_FF // LANES, H, LANES), F32)

    x = x_ref[...]
    if final_layer:
        x = x + branch_ref[...].astype(F32)
    xn = (x * _rms_scale(x) * g_ref[...]).astype(BF16)
    per = FFN_COLS // LANES

    def up(c):
        for base in (0, D_FF):
            u = _dot(xn, wup_ref[:, base + c * FFN_COLS:base + (c + 1) * FFN_COLS])
            for s in range(per):
                ubuf_ref[base // LANES + c * per + s, H:H + T, :] = u[:, s * LANES:(s + 1) * LANES]

    def conv(k, parity):
        cs = slice(k * LANES, (k + 1) * LANES)
        taps = [ubuf_ref[k, pl.ds(H + parity - 2 + i, half, stride=2), :] for i in range(3)]
        return (taps[0] * cw_ref[0:1, cs] + taps[1] * cw_ref[1:2, cs] + taps[2] * cw_ref[2:3, cs]
                + cb_ref[:, cs])

    def activate(c):
        for j in range(c * per, (c + 1) * per):
            for parity in range(2):
                gate = conv(j, parity)
                val = conv(D_FF // LANES + j, parity)
                act_ref[parity * half:(parity + 1) * half, j * LANES:(j + 1) * LANES] = (
                    gate * jax.nn.sigmoid(gate) * val).astype(BF16)

    nchunk = D_FF // FFN_COLS
    up(0)
    for c in range(nchunk):
        if c + 1 < nchunk:
            up(c + 1)
        activate(c)

    ubuf_ref[:, 0:H, :] = ubuf_ref[:, T:T + H, :]
    down = _dot(act_ref[...], wdn_ref[...])
    for k in range(nslab):
        for parity in range(2):
            slab_ref[k, pl.ds(parity, half, stride=2), :] = (
                down[parity * half:(parity + 1) * half, k * LANES:(k + 1) * LANES])
    y = x + jnp.concatenate([slab_ref[k] for k in range(nslab)], axis=1)
    if final_layer:
        out_ref[...] = y * _rms_scale(y) * fg_ref[...]
        return
    out_ref[...] = y
    xh = y * _rms_scale(y)
    xh0_ref[...] = xh.astype(BF16)
    for k in range(nslab):
        slab_ref[k] = xh[:, k * LANES:(k + 1) * LANES]
    d1, d2 = B_GROUPS[1][1], B_GROUPS[2][1]
    sub = d2 // d1
    for r1 in range(d1):
        for k in range(nslab):
            rows = slab_ref[k, pl.ds(r1, T // d1, stride=d1), :]
            cls_ref[r1 * nslab + k] = rows
            c0 = r1 * D_MODEL + k * LANES
            xh1_ref[:, c0:c0 + LANES] = rows.astype(BF16)
    for r1 in range(d1):
        for q in range(sub):
            for k in range(nslab):
                c0 = (r1 + d1 * q) * D_MODEL + k * LANES
                xh2_ref[:, c0:c0 + LANES] = (
                    cls_ref[r1 * nslab + k, pl.ds(q, T // d2, stride=sub), :].astype(BF16))


def _conv_ffn(x, layer, norm_g, w_up, conv_w, conv_b, w_down, branch=None, final_g=None):
    B, S, D = x.shape
    T = FFN_ROWS
    final_layer = branch is not None
    row_spec = pl.BlockSpec((None, T, D), lambda b, t: (b, t, 0))

    def layer_spec(*shape):
        return pl.BlockSpec((None,) + shape, lambda *_: (layer,) + (0,) * len(shape),
                            pipeline_mode=pl.Buffered(1))

    weight_specs = [layer_spec(1, D), layer_spec(D, 2 * D_FF), layer_spec(3, 2 * D_FF),
                    layer_spec(1, 2 * D_FF), layer_spec(D_FF, D)]
    weights = [norm_g[:, None, :], w_up, conv_w, conv_b[:, None, :], w_down]
    out_shape = [jax.ShapeDtypeStruct((B, S, D), F32)]
    out_specs = [row_spec]
    scratch = [pltpu.VMEM((2 * D_FF // LANES, T + CONV_HALO, LANES), F32),
               pltpu.VMEM((T, D_FF), BF16), pltpu.VMEM((D // LANES, T, LANES), F32)]
    if final_layer:
        in_specs = [row_spec, row_spec] + weight_specs + [_const_spec((1, D))]
        operands = [x, branch] + weights + [final_g[None, :]]
    else:
        in_specs = [row_spec] + weight_specs
        operands = [x] + weights
        out_shape.append(jax.ShapeDtypeStruct((B, S, D), BF16))
        out_specs.append(row_spec)
        for _, dil in B_GROUPS[1:]:
            out_shape.append(jax.ShapeDtypeStruct((B, S // dil, dil * D), BF16))
            out_specs.append(pl.BlockSpec((None, T // dil, dil * D), lambda b, t: (b, t, 0)))
        d1 = B_GROUPS[1][1]
        scratch.append(pltpu.VMEM((d1 * D // LANES, T // d1, LANES), F32))
    res = pl.pallas_call(
        functools.partial(_ffn_kernel, final_layer=final_layer),
        out_shape=out_shape,
        grid=(B, S // T),
        in_specs=in_specs,
        out_specs=out_specs,
        scratch_shapes=scratch,
        compiler_params=pltpu.CompilerParams(
            dimension_semantics=("arbitrary", "arbitrary"), vmem_limit_bytes=VMEM_LIMIT),
        name="conv_ffn_final" if final_layer else "conv_ffn",
    )(*operands)
    return res[0] if final_layer else res


def _attn_kernel(xh_ref, wq_ref, wkt_ref, wv_ref, band_ref, o_ref, lse_ref,
                 q2_ref, kt_ref, v_ref, bias_ref):
    T = xh_ref.shape[0]
    P = B_BLOCK
    i = pl.program_id(2)

    @pl.when((pl.program_id(0) == 0) & (pl.program_id(1) == 0) & (i == 0))
    def _():
        prev_key = lax.broadcasted_iota(jnp.int32, (2 * P, 2 * P), 1) < P
        for hp in range(B_HEADS // 2):
            bias_ref[0, hp] = band_ref[hp]
            bias_ref[1, hp] = jnp.where(prev_key, MASK_VALUE, band_ref[hp])

    @pl.when(i == 0)
    def _():
        kt_ref[:, 0:P] = jnp.zeros((D_MODEL, P), BF16)
        v_ref[0:P, :] = jnp.zeros((P, D_MODEL), BF16)

    @pl.when(i > 0)
    def _():
        kt_ref[:, 0:P] = kt_ref[:, T:T + P]
        v_ref[0:P, :] = v_ref[T:T + P, :]

    lane = lax.broadcasted_iota(jnp.int32, (P, D_MODEL), 1)
    even_head = (lane % (2 * B_HEAD_DIM)) < B_HEAD_DIM
    lane2 = lax.broadcasted_iota(jnp.int32, (P, 2 * B_HEAD_DIM), 1)
    first_half = lane2 < B_HEAD_DIM
    ones = jnp.ones((2 * P, LANES), BF16)
    first = (i == 0).astype(jnp.int32)

    xh = xh_ref[...]
    q = _dot(xh, wq_ref[...]) * (B_HEAD_DIM ** -0.5)
    for j in range(T // P):
        qj = q[j * P:(j + 1) * P]
        q2_ref[j, 0:P, :] = jnp.where(even_head, qj, 0.0).astype(BF16)
        q2_ref[j, P:2 * P, :] = jnp.where(even_head, 0.0, qj).astype(BF16)
    kt_ref[:, P:P + T] = _dot_nt(wkt_ref[...], xh).astype(BF16)
    v_ref[P:P + T, :] = _dot(xh, wv_ref[...]).astype(BF16)

    for j in range(T // P):
        lse_all = jnp.zeros((P, LANES), F32)
        for hp in range(B_HEADS // 2):
            cs = slice(hp * 2 * B_HEAD_DIM, (hp + 1) * 2 * B_HEAD_DIM)
            s = (_dot(q2_ref[j, :, cs], kt_ref[cs, j * P:(j + 2) * P])
                 + bias_ref[first if j == 0 else 0, hp])
            m = jnp.max(s, axis=-1, keepdims=True)
            p = jnp.exp(s - m).astype(BF16)
            va = jnp.concatenate([v_ref[j * P:(j + 2) * P, cs], ones], axis=1)
            pv = _dot(p, va)
            l = pv[:, LANES:]
            o = pv[:, :LANES] * (1.0 / l)
            lse = m + jnp.log(l)
            o_ref[j * P:(j + 1) * P, cs] = jnp.where(first_half, o[0:P], o[P:2 * P]).astype(BF16)
            lse_all = jnp.where(lane2 == 2 * hp, lse[0:P],
                                jnp.where(lane2 == 2 * hp + 1, lse[P:2 * P], lse_all))
        lse_ref[j * P:(j + 1) * P, :] = lse_all


def _t5_bucket(dist):
    max_exact = REL_BUCKETS // 2
    d = np.maximum(dist, 0)
    log_ratio = np.log(np.maximum(d, 1) / max_exact) / math.log(REL_MAX_DIST / max_exact)
    large = np.minimum(max_exact + (log_ratio * (REL_BUCKETS - max_exact)).astype(np.int64),
                       REL_BUCKETS - 1)
    return np.where(d < max_exact, d, large).astype(np.int32)


def _band_bias(rel_bias):
    P = B_BLOCK
    n = 3 * P
    t = np.arange(n)
    delta = P - np.where(t < 2 * P, t, t - n)
    vecs = []
    for g, (win, dil) in enumerate(B_GROUPS):
        valid = (delta >= 0) & (delta <= win // dil)
        vec = rel_bias[_t5_bucket(delta * dil)][:, g * B_HEADS:(g + 1) * B_HEADS].astype(F32)
        vecs.append(jnp.where(valid[:, None], vec, MASK_VALUE).T)
    vec = jnp.concatenate(vecs)
    bias = jnp.tile(vec, (1, P))[:, :P * (n - 1)].reshape(len(vecs) * B_HEADS, P, n - 1)
    return bias[:, :, :2 * P].reshape(len(vecs), B_HEADS // 2, 2 * P, 2 * P)


def _attn_group(xh, g, dil, wq_all, wkt_all, wkv_all, bias_all):
    B, n, _ = xh.shape
    D = D_MODEL
    T = ATT_ROWS
    cls_spec = pl.BlockSpec((None, T, D), lambda b, r, i: (b, i, r))
    lse_spec = pl.BlockSpec((None, T, LANES), lambda b, r, i: (b, i, r))

    def weight_spec(row_blk, col_blk):
        return pl.BlockSpec((D, D), lambda *_: (row_blk, col_blk), pipeline_mode=pl.Buffered(1))

    return pl.pallas_call(
        _attn_kernel,
        out_shape=[jax.ShapeDtypeStruct((B, n, dil * D), BF16),
                   jax.ShapeDtypeStruct((B, n, dil * LANES), F32)],
        grid=(B, dil, n // T),
        in_specs=[cls_spec, weight_spec(0, g), weight_spec(g, 0), weight_spec(0, len(B_GROUPS) + g),
                  pl.BlockSpec((None,) + bias_all.shape[1:], lambda *_: (g, 0, 0, 0),
                               pipeline_mode=pl.Buffered(1))],
        out_specs=[cls_spec, lse_spec],
        scratch_shapes=[pltpu.VMEM((T // B_BLOCK, 2 * B_BLOCK, D), BF16),
                        pltpu.VMEM((D, T + B_BLOCK), BF16), pltpu.VMEM((T + B_BLOCK, D), BF16),
                        pltpu.VMEM((2,) + bias_all.shape[1:], F32)],
        compiler_params=pltpu.CompilerParams(
            dimension_semantics=("arbitrary", "arbitrary", "arbitrary"),
            vmem_limit_bytes=VMEM_LIMIT),
        name=f"dilated_attn_g{g}",
    )(xh, wq_all, wkt_all, wkv_all, bias_all)


def _merge_kernel(o0_ref, o1_ref, o2_ref, l0_ref, l1_ref, l2_ref, ex_ref, wo_ref, out_ref,
                  os1_ref, os2_ref, ls1_ref, ls2_ref, mg_ref):
    T = o0_ref.shape[0]
    nslab = D_MODEL // LANES
    dilated = ((B_GROUPS[1][1], o1_ref, os1_ref, l1_ref, ls1_ref),
               (B_GROUPS[2][1], o2_ref, os2_ref, l2_ref, ls2_ref))
    for dil, _, _, l_src, l_dst in dilated:
        for r in range(dil):
            l_dst[pl.ds(r, T // dil, stride=dil), :] = l_src[:, r * LANES:(r + 1) * LANES]
    l0, l1, l2 = l0_ref[...], ls1_ref[...], ls2_ref[...]
    m = jnp.maximum(jnp.maximum(l0, l1), l2)
    e0, e1, e2 = jnp.exp(l0 - m), jnp.exp(l1 - m), jnp.exp(l2 - m)
    inv = 1.0 / (e0 + e1 + e2)

    def spread(w):
        return _dot(w.astype(BF16), ex_ref[...])

    w0, w1 = spread(e0 * inv), spread(e1 * inv)

    for k in range(nslab):
        for dil, o_src, o_dst, _, _ in dilated:
            for r in range(dil):
                c0 = r * D_MODEL + k * LANES
                o_dst[k, pl.ds(r, T // dil, stride=dil), :] = o_src[:, c0:c0 + LANES].astype(F32)
        ks = slice(k * LANES, (k + 1) * LANES)
        o2 = os2_ref[k]
        mg = o2 + w0[:, ks] * (o0_ref[:, ks].astype(F32) - o2) + w1[:, ks] * (os1_ref[k] - o2)
        mg_ref[:, ks] = mg.astype(BF16)
    out_ref[...] = _dot(mg_ref[...], wo_ref[...]).astype(BF16)


def _merge_groups(outs, lses, w_out):
    B, S, D = outs[0].shape
    T = MERGE_ROWS
    row_spec = pl.BlockSpec((None, T, D), lambda b, t: (b, t, 0))
    o_specs = [pl.BlockSpec((None, T // dil, dil * D), lambda b, t: (b, t, 0)) for _, dil in B_GROUPS]
    l_specs = [pl.BlockSpec((None, T // dil, dil * LANES), lambda b, t: (b, t, 0))
               for _, dil in B_GROUPS]
    expand = np.zeros((LANES, D), np.float32)
    for h in range(B_HEADS):
        expand[h, h * B_HEAD_DIM:(h + 1) * B_HEAD_DIM] = 1.0
    nslab = D // LANES
    return pl.pallas_call(
        _merge_kernel,
        out_shape=jax.ShapeDtypeStruct((B, S, D), BF16),
        grid=(B, S // T),
        in_specs=o_specs + l_specs + [_const_spec((LANES, D)), _const_spec((D, D))],
        out_specs=row_spec,
        scratch_shapes=[pltpu.VMEM((nslab, T, LANES), F32), pltpu.VMEM((nslab, T, LANES), F32),
                        pltpu.VMEM((T, LANES), F32), pltpu.VMEM((T, LANES), F32),
                        pltpu.VMEM((T, D), BF16)],
        compiler_params=pltpu.CompilerParams(
            dimension_semantics=("arbitrary", "arbitrary"), vmem_limit_bytes=VMEM_LIMIT),
        name="merge_groups",
    )(*outs, *lses, jnp.asarray(expand, BF16), w_out.astype(BF16))


def kernel(x, a_norm_g, a_w_in, a_b_if, a_hnorm_g, a_w_out, kv_norm_g, w_kv, b_norm_g, b_w_q,
           b_w_out, rel_bias, f_norm_g, f_w_up, f_conv_w, f_conv_b, f_w_down, final_norm_g):
    x = _mlstm_layer(x, a_norm_g[0], a_w_in[0], a_b_if[0], a_hnorm_g[0], a_w_out[0])
    ffn_params = (f_norm_g, f_w_up.astype(BF16), f_conv_w, f_conv_b, f_w_down.astype(BF16))
    x, *streams = _conv_ffn(x, 0, *ffn_params)
    att_w = len(B_GROUPS) * B_HEADS * B_HEAD_DIM
    wq_all = (b_norm_g[0][:, None] * b_w_q[0]).astype(BF16)
    wkv_all = (kv_norm_g[:, None] * w_kv).astype(BF16)
    wkt_all = wkv_all[:, :att_w].T
    bias_all = _band_bias(rel_bias)
    outs, lses = [], []
    for g, (_, dil) in enumerate(B_GROUPS):
        o, lse = _attn_group(streams[g], g, dil, wq_all, wkt_all, wkv_all, bias_all)
        outs.append(o)
        lses.append(lse)
    branch = _merge_groups(outs, lses, b_w_out[0])
    return _conv_ffn(x, 1, *ffn_params, branch=branch, final_g=final_norm_g)
```

```python
import functools
import math

import numpy as np
import jax
import jax.numpy as jnp
from jax import lax
from jax.experimental import pallas as pl
from jax.experimental.pallas import tpu as pltpu

F32 = jnp.float32
BF16 = jnp.bfloat16

D_MODEL = 1024
A_HEADS = 4
A_QK_DIM = 128
A_V_DIM = 256
A_NQ = A_HEADS * A_QK_DIM
A_NV = A_HEADS * A_V_DIM
A_Z_DIM = 2 * A_NQ + 2 * A_NV
GATE_SOFTCAP = 15.0
B_GROUPS = ((128, 1), (512, 4), (2048, 16))
B_HEAD_DIM = 64
B_HEADS = 16
B_BLOCK = 128
REL_BUCKETS = 32
REL_MAX_DIST = 2048
D_FF = 2816
EPS = 1e-6
MASK_VALUE = -1e30

LANES = 128
SUBLANES = 8
BF16_ROWS = 2 * SUBLANES
MXU_TILE = 256
V7X_VMEM_BYTES = 64 * 1024 * 1024

GATE_LANES = LANES
GATE_ROWS = BF16_ROWS
MLSTM_CHUNK = 256
FFN_ROWS = 512
FFN_COLS = MXU_TILE
CONV_HALO = SUBLANES
ATT_ROWS = 1024
MERGE_ROWS = 1024
VMEM_LIMIT = V7X_VMEM_BYTES * 7 // 8


def _rms_scale(x):
    return lax.rsqrt(jnp.mean(x * x, axis=-1, keepdims=True) + EPS)


def _softcap(z):
    return GATE_SOFTCAP * jnp.tanh(z / GATE_SOFTCAP)


def _log_sigmoid(a):
    return jnp.minimum(a, 0.0) - jnp.log1p(jnp.exp(-jnp.abs(a)))


def _split3(v):
    hi = v.astype(BF16)
    r1 = v - hi.astype(F32)
    mid = r1.astype(BF16)
    lo = (r1 - mid.astype(F32)).astype(BF16)
    return hi, mid, lo


def _dot(a, b):
    return jnp.dot(a, b, preferred_element_type=F32)


def _dot_nt(a, b):
    return lax.dot_general(a, b, (((1,), (1,)), ((), ())), preferred_element_type=F32)


def _dot_tn(a, b):
    return lax.dot_general(a, b, (((0,), (0,)), ((), ())), preferred_element_type=F32)


def _const_spec(shape):
    return pl.BlockSpec(shape, lambda *_: (0,) * len(shape), pipeline_mode=pl.Buffered(1))


def _mlstm_kernel(x_ref, g_ref, wz_ref, wgr_ref, bc_ref, br_ref, hg_ref, wo_ref,
                  out_ref, c_ref, n_ref, m_ref, hcat_ref):
    NB, L = x_ref.shape[0], x_ref.shape[1]

    @pl.when(pl.program_id(0) == 0)
    def _():
        c_ref[...] = jnp.zeros_like(c_ref)
        n_ref[...] = jnp.zeros_like(n_ref)
        m_ref[...] = jnp.zeros_like(m_ref)

    row = lax.broadcasted_iota(jnp.int32, (L, L), 0)
    col = lax.broadcasted_iota(jnp.int32, (L, L), 1)
    causal = col <= row
    tril = jnp.where(causal, 1.0, 0.0).astype(BF16)
    triu = jnp.where(row <= col, 1.0, 0.0).astype(BF16)

    def projection(b):
        x = x_ref[b]
        xn = (x * _rms_scale(x) * g_ref[...]).astype(BF16)
        z = {}

        def gates():
            z["ac"] = _softcap(_dot(xn, wz_ref[:, A_Z_DIM:]) + bc_ref[...])
            z["ar"] = _softcap(_dot_nt(wgr_ref[...], xn) + br_ref[...])
            z["bcs"] = sum(_dot(tril, p) for p in _split3(_log_sigmoid(z["ac"])))
            z["brs"] = sum(_dot(p, triu) for p in _split3(_log_sigmoid(z["ar"])))

        def columns(name, c0, c1):
            z[name] = _dot(xn, wz_ref[:, c0:c1])

        return z, [gates,
                   functools.partial(columns, "qk", 0, 2 * A_NQ),
                   functools.partial(columns, "v", 2 * A_NQ, 2 * A_NQ + A_NV),
                   functools.partial(columns, "o", 2 * A_NQ + A_NV, A_Z_DIM)]

    def head(h, b, z):
        st = b * A_HEADS + h
        q = z["qk"][:, h * A_QK_DIM:(h + 1) * A_QK_DIM] * (A_QK_DIM ** -0.5)
        k = z["qk"][:, A_NQ + h * A_QK_DIM:A_NQ + (h + 1) * A_QK_DIM]
        v = z["v"][:, h * A_V_DIM:(h + 1) * A_V_DIM]
        o = z["o"][:, h * A_V_DIM:(h + 1) * A_V_DIM]
        qb, kb, vb = q.astype(BF16), k.astype(BF16), v.astype(BF16)
        li_c = z["ac"][:, h:h + 1]
        li_r = z["ar"][h:h + 1, :]
        b_c = z["bcs"][:, A_HEADS + h:A_HEADS + h + 1]
        b_r = z["brs"][A_HEADS + h:A_HEADS + h + 1, :]
        m_prev = m_ref[st]
        c_prev = c_ref[st]
        n_prev = n_ref[st]

        log_d = jnp.where(causal, b_c - b_r + li_r, -jnp.inf)
        m_inter = b_c + m_prev
        m_t = jnp.maximum(m_inter, jnp.max(log_d, axis=-1, keepdims=True))
        sm = _dot_nt(qb, kb) * jnp.exp(log_d - m_t)
        w_inter = jnp.exp(m_inter - m_t)
        num = _dot(sm.astype(BF16), vb) + w_inter * _dot(qb, c_prev.astype(BF16))
        den = (jnp.sum(sm, axis=-1, keepdims=True)
               + w_inter * jnp.sum(q * n_prev, axis=-1, keepdims=True))
        hv = num * (1.0 / jnp.maximum(jnp.abs(den), jnp.exp(-m_t)))
        hv = hv * _rms_scale(hv) * hg_ref[:, h * A_V_DIM:(h + 1) * A_V_DIM]
        hcat_ref[b, :, h * A_V_DIM:(h + 1) * A_V_DIM] = (hv * jax.nn.sigmoid(o)).astype(BF16)

        b_last = b_c[L - 1:L, :]
        g_c = b_last - b_c + li_c
        g_r = b_last - b_r + li_r
        m_new = jnp.maximum(b_last + m_prev, jnp.max(g_r, axis=-1, keepdims=True))
        decay = jnp.exp(b_last + m_prev - m_new)
        kw = k * jnp.exp(g_c - m_new)
        c_ref[st] = decay * c_prev + _dot_tn(kw.astype(BF16), vb)
        n_ref[st] = decay * n_prev + jnp.sum(kw, axis=0, keepdims=True)
        m_ref[st] = m_new

    z_prev = None
    for b in range(NB):
        z, pieces = projection(b)
        for h in range(A_HEADS):
            if z_prev is not None:
                head(h, b - 1, z_prev)
            pieces[h]()
        z_prev = z
    for h in range(A_HEADS):
        head(h, NB - 1, z_prev)

    for b in range(NB):
        out_ref[b] = x_ref[b] + _dot(hcat_ref[b], wo_ref[...])


def _mlstm_layer(x, norm_g, w_in, b_if, hnorm_g, w_out):
    B, S, D = x.shape
    L = MLSTM_CHUNK
    wg = w_in[:, A_Z_DIM:]
    ng = 2 * A_HEADS
    wz = jnp.pad(w_in, ((0, 0), (0, GATE_LANES - ng))).astype(BF16)
    wgr = jnp.pad(wg.T, ((0, GATE_ROWS - ng), (0, 0))).astype(BF16)
    bc = jnp.pad(b_if[None, :], ((0, 0), (0, GATE_LANES - ng)))
    br = jnp.pad(b_if[:, None], ((0, GATE_ROWS - ng), (0, 0)))
    row_spec = pl.BlockSpec((B, L, D), lambda c: (0, c, 0))
    return pl.pallas_call(
        _mlstm_kernel,
        out_shape=jax.ShapeDtypeStruct((B, S, D), F32),
        grid=(S // L,),
        in_specs=[row_spec, _const_spec((1, D)), _const_spec((D, A_Z_DIM + GATE_LANES)),
                  _const_spec((GATE_ROWS, D)),
                  _const_spec((1, GATE_LANES)), _const_spec((GATE_ROWS, 1)),
                  _const_spec((1, A_NV)), _const_spec((A_NV, D))],
        out_specs=row_spec,
        scratch_shapes=[pltpu.VMEM((B * A_HEADS, A_QK_DIM, A_V_DIM), F32),
                        pltpu.VMEM((B * A_HEADS, 1, A_QK_DIM), F32),
                        pltpu.VMEM((B * A_HEADS, 1, 1), F32),
                        pltpu.VMEM((B, L, A_NV), BF16)],
        compiler_params=pltpu.CompilerParams(
            dimension_semantics=("arbitrary",), vmem_limit_bytes=VMEM_LIMIT),
        name="mlstm_layer",
    )(x, norm_g[None, :], wz, wgr, bc, br, hnorm_g.reshape(1, A_NV), w_out.astype(BF16))


def _ffn_kernel(x_ref, *rest, final_layer):
    if final_layer:
        (branch_ref, g_ref, wup_ref, cw_ref, cb_ref, wdn_ref, fg_ref,
         out_ref, ubuf_ref, act_ref, slab_ref) = rest
    else:
        (g_ref, wup_ref, cw_ref, cb_ref, wdn_ref,
         out_ref, xh0_ref, xh1_ref, xh2_ref, ubuf_ref, act_ref, slab_ref, cls_ref) = rest
    T = x_ref.shape[0]
    H = CONV_HALO
    half = T // 2
    nslab = D_MODEL // LANES

    @pl.when(pl.program_id(1) == 0)
    def _():
        ubuf_ref[:, 0:H, :] = jnp.zeros((2 * D_FF // LANES, H, LANES), F32)

    x = x_ref[...]
    if final_layer:
        x = x + branch_ref[...].astype(F32)
    xn = (x * _rms_scale(x) * g_ref[...]).astype(BF16)
    per = FFN_COLS // LANES

    def up(c):
        for base in (0, D_FF):
            u = _dot(xn, wup_ref[:, base + c * FFN_COLS:base + (c + 1) * FFN_COLS])
            for s in range(per):
                ubuf_ref[base // LANES + c * per + s, H:H + T, :] = u[:, s * LANES:(s + 1) * LANES]

    def conv(k, parity):
        cs = slice(k * LANES, (k + 1) * LANES)
        taps = [ubuf_ref[k, pl.ds(H + parity - 2 + i, half, stride=2), :] for i in range(3)]
        return (taps[0] * cw_ref[0:1, cs] + taps[1] * cw_ref[1:2, cs] + taps[2] * cw_ref[2:3, cs]
                + cb_ref[:, cs])

    def activate(c):
        for j in range(c * per, (c + 1) * per):
            for parity in range(2):
                gate = conv(j, parity)
                val = conv(D_FF // LANES + j, parity)
                act_ref[parity * half:(parity + 1) * half, j * LANES:(j + 1) * LANES] = (
                    gate * jax.nn.sigmoid(gate) * val).astype(BF16)

    nchunk = D_FF // FFN_COLS
    up(0)
    for c in range(nchunk):
        if c + 1 < nchunk:
            up(c + 1)
        activate(c)

    ubuf_ref[:, 0:H, :] = ubuf_ref[:, T:T + H, :]
    down = _dot(act_ref[...], wdn_ref[...])
    for k in range(nslab):
        for parity in range(2):
            slab_ref[k, pl.ds(parity, half, stride=2), :] = (
                down[parity * half:(parity + 1) * half, k * LANES:(k + 1) * LANES])
    y = x + jnp.concatenate([slab_ref[k] for k in range(nslab)], axis=1)
    if final_layer:
        out_ref[...] = y * _rms_scale(y) * fg_ref[...]
        return
    out_ref[...] = y
    xh = y * _rms_scale(y)
    xh0_ref[...] = xh.astype(BF16)
    for k in range(nslab):
        slab_ref[k] = xh[:, k * LANES:(k + 1) * LANES]
    d1, d2 = B_GROUPS[1][1], B_GROUPS[2][1]
    sub = d2 // d1
    for r1 in range(d1):
        for k in range(nslab):
            rows = slab_ref[k, pl.ds(r1, T // d1, stride=d1), :]
            cls_ref[r1 * nslab + k] = rows
            c0 = r1 * D_MODEL + k * LANES
            xh1_ref[:, c0:c0 + LANES] = rows.astype(BF16)
    for r1 in range(d1):
        for q in range(sub):
            for k in range(nslab):
                c0 = (r1 + d1 * q) * D_MODEL + k * LANES
                xh2_ref[:, c0:c0 + LANES] = (
                    cls_ref[r1 * nslab + k, pl.ds(q, T // d2, stride=sub), :].astype(BF16))


def _conv_ffn(x, layer, norm_g, w_up, conv_w, conv_b, w_down, branch=None, final_g=None):
    B, S, D = x.shape
    T = FFN_ROWS
    final_layer = branch is not None
    row_spec = pl.BlockSpec((None, T, D), lambda b, t: (b, t, 0))

    def layer_spec(*shape):
        return pl.BlockSpec((None,) + shape, lambda *_: (layer,) + (0,) * len(shape),
                            pipeline_mode=pl.Buffered(1))

    weight_specs = [layer_spec(1, D), layer_spec(D, 2 * D_FF), layer_spec(3, 2 * D_FF),
                    layer_spec(1, 2 * D_FF), layer_spec(D_FF, D)]
    weights = [norm_g[:, None, :], w_up, conv_w, conv_b[:, None, :], w_down]
    out_shape = [jax.ShapeDtypeStruct((B, S, D), F32)]
    out_specs = [row_spec]
    scratch = [pltpu.VMEM((2 * D_FF // LANES, T + CONV_HALO, LANES), F32),
               pltpu.VMEM((T, D_FF), BF16), pltpu.VMEM((D // LANES, T, LANES), F32)]
    if final_layer:
        in_specs = [row_spec, row_spec] + weight_specs + [_const_spec((1, D))]
        operands = [x, branch] + weights + [final_g[None, :]]
    else:
        in_specs = [row_spec] + weight_specs
        operands = [x] + weights
        out_shape.append(jax.ShapeDtypeStruct((B, S, D), BF16))
        out_specs.append(row_spec)
        for _, dil in B_GROUPS[1:]:
            out_shape.append(jax.ShapeDtypeStruct((B, S // dil, dil * D), BF16))
            out_specs.append(pl.BlockSpec((None, T // dil, dil * D), lambda b, t: (b, t, 0)))
        d1 = B_GROUPS[1][1]
        scratch.append(pltpu.VMEM((d1 * D // LANES, T // d1, LANES), F32))
    res = pl.pallas_call(
        functools.partial(_ffn_kernel, final_layer=final_layer),
        out_shape=out_shape,
        grid=(B, S // T),
        in_specs=in_specs,
        out_specs=out_specs,
        scratch_shapes=scratch,
        compiler_params=pltpu.CompilerParams(
            dimension_semantics=("arbitrary", "arbitrary"), vmem_limit_bytes=VMEM_LIMIT),
        name="conv_ffn_final" if final_layer else "conv_ffn",
    )(*operands)
    return res[0] if final_layer else res


def _attn_kernel(xh_ref, wq_ref, wkt_ref, wv_ref, band_ref, o_ref, lse_ref,
                 q2_ref, kt_ref, v_ref, bias_ref):
    T = xh_ref.shape[0]
    P = B_BLOCK
    i = pl.program_id(2)

    @pl.when((pl.program_id(0) == 0) & (pl.program_id(1) == 0) & (i == 0))
    def _():
        prev_key = lax.broadcasted_iota(jnp.int32, (2 * P, 2 * P), 1) < P
        for hp in range(B_HEADS // 2):
            bias_ref[0, hp] = band_ref[hp]
            bias_ref[1, hp] = jnp.where(prev_key, MASK_VALUE, band_ref[hp])

    @pl.when(i == 0)
    def _():
        kt_ref[:, 0:P] = jnp.zeros((D_MODEL, P), BF16)
        v_ref[0:P, :] = jnp.zeros((P, D_MODEL), BF16)

    @pl.when(i > 0)
    def _():
        kt_ref[:, 0:P] = kt_ref[:, T:T + P]
        v_ref[0:P, :] = v_ref[T:T + P, :]

    lane = lax.broadcasted_iota(jnp.int32, (P, D_MODEL), 1)
    even_head = (lane % (2 * B_HEAD_DIM)) < B_HEAD_DIM
    lane2 = lax.broadcasted_iota(jnp.int32, (P, 2 * B_HEAD_DIM), 1)
    first_half = lane2 < B_HEAD_DIM
    ones = jnp.ones((2 * P, LANES), BF16)
    first = (i == 0).astype(jnp.int32)

    xh = xh_ref[...]
    q = _dot(xh, wq_ref[...]) * (B_HEAD_DIM ** -0.5)
    for j in range(T // P):
        qj = q[j * P:(j + 1) * P]
        q2_ref[j, 0:P, :] = jnp.where(even_head, qj, 0.0).astype(BF16)
        q2_ref[j, P:2 * P, :] = jnp.where(even_head, 0.0, qj).astype(BF16)
    kt_ref[:, P:P + T] = _dot_nt(wkt_ref[...], xh).astype(BF16)
    v_ref[P:P + T, :] = _dot(xh, wv_ref[...]).astype(BF16)

    for j in range(T // P):
        lse_all = jnp.zeros((P, LANES), F32)
        for hp in range(B_HEADS // 2):
            cs = slice(hp * 2 * B_HEAD_DIM, (hp + 1) * 2 * B_HEAD_DIM)
            s = (_dot(q2_ref[j, :, cs], kt_ref[cs, j * P:(j + 2) * P])
                 + bias_ref[first if j == 0 else 0, hp])
            m = jnp.max(s, axis=-1, keepdims=True)
            p = jnp.exp(s - m).astype(BF16)
            va = jnp.concatenate([v_ref[j * P:(j + 2) * P, cs], ones], axis=1)
            pv = _dot(p, va)
            l = pv[:, LANES:]
            o = pv[:, :LANES] * (1.0 / l)
            lse = m + jnp.log(l)
            o_ref[j * P:(j + 1) * P, cs] = jnp.where(first_half, o[0:P], o[P:2 * P]).astype(BF16)
            lse_all = jnp.where(lane2 == 2 * hp, lse[0:P],
                                jnp.where(lane2 == 2 * hp + 1, lse[P:2 * P], lse_all))
        lse_ref[j * P:(j + 1) * P, :] = lse_all


def _t5_bucket(dist):
    max_exact = REL_BUCKETS // 2
    d = np.maximum(dist, 0)
    log_ratio = np.log(np.maximum(d, 1) / max_exact) / math.log(REL_MAX_DIST / max_exact)
    large = np.minimum(max_exact + (log_ratio * (REL_BUCKETS - max_exact)).astype(np.int64),
                       REL_BUCKETS - 1)
    return np.where(d < max_exact, d, large).astype(np.int32)


def _band_bias(rel_bias):
    P = B_BLOCK
    n = 3 * P
    t = np.arange(n)
    delta = P - np.where(t < 2 * P, t, t - n)
    vecs = []
    for g, (win, dil) in enumerate(B_GROUPS):
        valid = (delta >= 0) & (delta <= win // dil)
        vec = rel_bias[_t5_bucket(delta * dil)][:, g * B_HEADS:(g + 1) * B_HEADS].astype(F32)
        vecs.append(jnp.where(valid[:, None], vec, MASK_VALUE).T)
    vec = jnp.concatenate(vecs)
    bias = jnp.tile(vec, (1, P))[:, :P * (n - 1)].reshape(len(vecs) * B_HEADS, P, n - 1)
    return bias[:, :, :2 * P].reshape(len(vecs), B_HEADS // 2, 2 * P, 2 * P)


def _attn_group(xh, g, dil, wq_all, wkt_all, wv_all, bias_all):
    B, n, _ = xh.shape
    D = D_MODEL
    T = ATT_ROWS
    cls_spec = pl.BlockSpec((None, T, D), lambda b, r, i: (b, i, r))
    lse_spec = pl.BlockSpec((None, T, LANES), lambda b, r, i: (b, i, r))

    def weight_spec(row_blk, col_blk):
        return pl.BlockSpec((D, D), lambda *_: (row_blk, col_blk), pipeline_mode=pl.Buffered(1))

    return pl.pallas_call(
        _attn_kernel,
        out_shape=[jax.ShapeDtypeStruct((B, n, dil * D), BF16),
                   jax.ShapeDtypeStruct((B, n, dil * LANES), F32)],
        grid=(B, dil, n // T),
        in_specs=[cls_spec, weight_spec(0, g), weight_spec(g, 0), weight_spec(0, g),
                  pl.BlockSpec((None,) + bias_all.shape[1:], lambda *_: (g, 0, 0, 0),
                               pipeline_mode=pl.Buffered(1))],
        out_specs=[cls_spec, lse_spec],
        scratch_shapes=[pltpu.VMEM((T // B_BLOCK, 2 * B_BLOCK, D), BF16),
                        pltpu.VMEM((D, T + B_BLOCK), BF16), pltpu.VMEM((T + B_BLOCK, D), BF16),
                        pltpu.VMEM((2,) + bias_all.shape[1:], F32)],
        compiler_params=pltpu.CompilerParams(
            dimension_semantics=("arbitrary", "arbitrary", "arbitrary"),
            vmem_limit_bytes=VMEM_LIMIT),
        name=f"dilated_attn_g{g}",
    )(xh, wq_all, wkt_all, wv_all, bias_all)


def _merge_kernel(o0_ref, o1_ref, o2_ref, l0_ref, l1_ref, l2_ref, ex_ref, wo_ref, out_ref,
                  os1_ref, os2_ref, ls1_ref, ls2_ref, mg_ref, mid_ref):
    T = o0_ref.shape[0]
    nslab = D_MODEL // LANES
    d1, d2 = B_GROUPS[1][1], B_GROUPS[2][1]
    sub = d2 // d1
    for dil, l_src, l_dst in ((d1, l1_ref, ls1_ref), (d2, l2_ref, ls2_ref)):
        for r in range(dil):
            l_dst[pl.ds(r, T // dil, stride=dil), :] = l_src[:, r * LANES:(r + 1) * LANES]
    l0, l1, l2 = l0_ref[...], ls1_ref[...], ls2_ref[...]
    m = jnp.maximum(jnp.maximum(l0, l1), l2)
    e0, e1, e2 = jnp.exp(l0 - m), jnp.exp(l1 - m), jnp.exp(l2 - m)
    inv = 1.0 / (e0 + e1 + e2)

    def spread(w):
        return _dot(w.astype(BF16), ex_ref[...])

    w0, w1 = spread(e0 * inv), spread(e1 * inv)

    for k in range(nslab):
        for r1 in range(d1):
            c0 = r1 * D_MODEL + k * LANES
            os1_ref[k, pl.ds(r1, T // d1, stride=d1), :] = o1_ref[:, c0:c0 + LANES].astype(F32)
            for q in range(sub):
                c0 = (r1 + d1 * q) * D_MODEL + k * LANES
                mid_ref[k * d1 + r1, pl.ds(q, T // d2, stride=sub), :] = (
                    o2_ref[:, c0:c0 + LANES].astype(F32))
            os2_ref[k, pl.ds(r1, T // d1, stride=d1), :] = mid_ref[k * d1 + r1]
        ks = slice(k * LANES, (k + 1) * LANES)
        o2 = os2_ref[k]
        mg = o2 + w0[:, ks] * (o0_ref[:, ks].astype(F32) - o2) + w1[:, ks] * (os1_ref[k] - o2)
        mg_ref[:, ks] = mg.astype(BF16)
    out_ref[...] = _dot(mg_ref[...], wo_ref[...]).astype(BF16)


def _merge_groups(outs, lses, w_out):
    B, S, D = outs[0].shape
    T = MERGE_ROWS
    row_spec = pl.BlockSpec((None, T, D), lambda b, t: (b, t, 0))
    o_specs = [pl.BlockSpec((None, T // dil, dil * D), lambda b, t: (b, t, 0)) for _, dil in B_GROUPS]
    l_specs = [pl.BlockSpec((None, T // dil, dil * LANES), lambda b, t: (b, t, 0))
               for _, dil in B_GROUPS]
    expand = np.zeros((LANES, D), np.float32)
    for h in range(B_HEADS):
        expand[h, h * B_HEAD_DIM:(h + 1) * B_HEAD_DIM] = 1.0
    nslab = D // LANES
    return pl.pallas_call(
        _merge_kernel,
        out_shape=jax.ShapeDtypeStruct((B, S, D), BF16),
        grid=(B, S // T),
        in_specs=o_specs + l_specs + [_const_spec((LANES, D)), _const_spec((D, D))],
        out_specs=row_spec,
        scratch_shapes=[pltpu.VMEM((nslab, T, LANES), F32), pltpu.VMEM((nslab, T, LANES), F32),
                        pltpu.VMEM((T, LANES), F32), pltpu.VMEM((T, LANES), F32),
                        pltpu.VMEM((T, D), BF16),
                        pltpu.VMEM((nslab * B_GROUPS[1][1], T // B_GROUPS[1][1], LANES), F32)],
        compiler_params=pltpu.CompilerParams(
            dimension_semantics=("arbitrary", "arbitrary"), vmem_limit_bytes=VMEM_LIMIT),
        name="merge_groups",
    )(*outs, *lses, jnp.asarray(expand, BF16), w_out.astype(BF16))


def kernel(x, a_norm_g, a_w_in, a_b_if, a_hnorm_g, a_w_out, kv_norm_g, w_kv, b_norm_g, b_w_q,
           b_w_out, rel_bias, f_norm_g, f_w_up, f_conv_w, f_conv_b, f_w_down, final_norm_g):
    x = _mlstm_layer(x, a_norm_g[0], a_w_in[0], a_b_if[0], a_hnorm_g[0], a_w_out[0])
    ffn_params = (f_norm_g, f_w_up.astype(BF16), f_conv_w, f_conv_b, f_w_down.astype(BF16))
    x, *streams = _conv_ffn(x, 0, *ffn_params)
    att_w = len(B_GROUPS) * B_HEADS * B_HEAD_DIM
    wq_all = (b_norm_g[0][:, None] * b_w_q[0]).astype(BF16)
    wkt_all = (kv_norm_g[:, None] * w_kv[:, :att_w]).T.astype(BF16)
    wv_all = (kv_norm_g[:, None] * w_kv[:, att_w:]).astype(BF16)
    bias_all = _band_bias(rel_bias)
    outs, lses = [], []
    for g, (_, dil) in enumerate(B_GROUPS):
        o, lse = _attn_group(streams[g], g, dil, wq_all, wkt_all, wv_all, bias_all)
        outs.append(o)
        lses.append(lse)
    branch = _merge_groups(outs, lses, b_w_out[0])
    return _conv_ffn(x, 1, *ffn_params, branch=branch, final_g=final_norm_g)
```

```python
import functools
import math

import numpy as np
import jax
import jax.numpy as jnp
from jax import lax
from jax.experimental import pallas as pl
from jax.experimental.pallas import tpu as pltpu

F32 = jnp.float32
BF16 = jnp.bfloat16

D_MODEL = 1024
A_HEADS = 4
A_QK_DIM = 128
A_V_DIM = 256
A_NQ = A_HEADS * A_QK_DIM
A_NV = A_HEADS * A_V_DIM
A_Z_DIM = 2 * A_NQ + 2 * A_NV
GATE_SOFTCAP = 15.0
B_GROUPS = ((128, 1), (512, 4), (2048, 16))
B_HEAD_DIM = 64
B_HEADS = 16
B_BLOCK = 128
REL_BUCKETS = 32
REL_MAX_DIST = 2048
D_FF = 2816
EPS = 1e-6
MASK_VALUE = -1e30

LANES = 128
SUBLANES = 8
BF16_ROWS = 2 * SUBLANES
MXU_TILE = 256
V7X_VMEM_BYTES = 64 * 1024 * 1024

GATE_LANES = LANES
GATE_ROWS = BF16_ROWS
MLSTM_CHUNK = 256
FFN_ROWS = 512
FFN_COLS = MXU_TILE
CONV_HALO = SUBLANES
ATT_ROWS = 1024
MERGE_ROWS = 1024
VMEM_LIMIT = V7X_VMEM_BYTES * 7 // 8


def _rms_scale(x):
    return lax.rsqrt(jnp.mean(x * x, axis=-1, keepdims=True) + EPS)


def _softcap(z):
    return GATE_SOFTCAP * jnp.tanh(z / GATE_SOFTCAP)


def _log_sigmoid(a):
    return jnp.minimum(a, 0.0) - jnp.log1p(jnp.exp(-jnp.abs(a)))


def _split3(v):
    hi = v.astype(BF16)
    r1 = v - hi.astype(F32)
    mid = r1.astype(BF16)
    lo = (r1 - mid.astype(F32)).astype(BF16)
    return hi, mid, lo


def _dot(a, b):
    return jnp.dot(a, b, preferred_element_type=F32)


def _dot_nt(a, b):
    return lax.dot_general(a, b, (((1,), (1,)), ((), ())), preferred_element_type=F32)


def _dot_tn(a, b):
    return lax.dot_general(a, b, (((0,), (0,)), ((), ())), preferred_element_type=F32)


def _const_spec(shape):
    return pl.BlockSpec(shape, lambda *_: (0,) * len(shape), pipeline_mode=pl.Buffered(1))


def _mlstm_kernel(x_ref, g_ref, wz_ref, wgr_ref, bc_ref, br_ref, hg_ref, wo_ref,
                  out_ref, c_ref, n_ref, m_ref, hcat_ref):
    NB, L = x_ref.shape[0], x_ref.shape[1]

    @pl.when(pl.program_id(0) == 0)
    def _():
        c_ref[...] = jnp.zeros_like(c_ref)
        n_ref[...] = jnp.zeros_like(n_ref)
        m_ref[...] = jnp.zeros_like(m_ref)

    row = lax.broadcasted_iota(jnp.int32, (L, L), 0)
    col = lax.broadcasted_iota(jnp.int32, (L, L), 1)
    causal = col <= row
    tril = jnp.where(causal, 1.0, 0.0).astype(BF16)
    triu = jnp.where(row <= col, 1.0, 0.0).astype(BF16)

    def projection(b):
        x = x_ref[b]
        xn = (x * _rms_scale(x) * g_ref[...]).astype(BF16)
        z = {}

        def gates():
            z["ac"] = _softcap(_dot(xn, wz_ref[:, A_Z_DIM:]) + bc_ref[...])
            z["ar"] = _softcap(_dot_nt(wgr_ref[...], xn) + br_ref[...])
            z["bcs"] = sum(_dot(tril, p) for p in _split3(_log_sigmoid(z["ac"])))
            z["brs"] = sum(_dot(p, triu) for p in _split3(_log_sigmoid(z["ar"])))

        def columns(name, c0, c1):
            z[name] = _dot(xn, wz_ref[:, c0:c1])

        return z, [gates,
                   functools.partial(columns, "qk", 0, 2 * A_NQ),
                   functools.partial(columns, "v", 2 * A_NQ, 2 * A_NQ + A_NV),
                   functools.partial(columns, "o", 2 * A_NQ + A_NV, A_Z_DIM)]

    def head(h, b, z):
        st = b * A_HEADS + h
        q = z["qk"][:, h * A_QK_DIM:(h + 1) * A_QK_DIM] * (A_QK_DIM ** -0.5)
        k = z["qk"][:, A_NQ + h * A_QK_DIM:A_NQ + (h + 1) * A_QK_DIM]
        v = z["v"][:, h * A_V_DIM:(h + 1) * A_V_DIM]
        o = z["o"][:, h * A_V_DIM:(h + 1) * A_V_DIM]
        qb, kb, vb = q.astype(BF16), k.astype(BF16), v.astype(BF16)
        li_c = z["ac"][:, h:h + 1]
        li_r = z["ar"][h:h + 1, :]
        b_c = z["bcs"][:, A_HEADS + h:A_HEADS + h + 1]
        b_r = z["brs"][A_HEADS + h:A_HEADS + h + 1, :]
        m_prev = m_ref[st]
        c_prev = c_ref[st]
        n_prev = n_ref[st]

        log_d = jnp.where(causal, b_c - b_r + li_r, -jnp.inf)
        m_inter = b_c + m_prev
        m_t = jnp.maximum(m_inter, jnp.max(log_d, axis=-1, keepdims=True))
        sm = _dot_nt(qb, kb) * jnp.exp(log_d - m_t)
        w_inter = jnp.exp(m_inter - m_t)
        num = _dot(sm.astype(BF16), vb) + w_inter * _dot(qb, c_prev.astype(BF16))
        den = (jnp.sum(sm, axis=-1, keepdims=True)
               + w_inter * jnp.sum(q * n_prev, axis=-1, keepdims=True))
        hv = num * (1.0 / jnp.maximum(jnp.abs(den), jnp.exp(-m_t)))
        hv = hv * _rms_scale(hv) * hg_ref[:, h * A_V_DIM:(h + 1) * A_V_DIM]
        hcat_ref[b, :, h * A_V_DIM:(h + 1) * A_V_DIM] = (hv * jax.nn.sigmoid(o)).astype(BF16)

        b_last = b_c[L - 1:L, :]
        g_c = b_last - b_c + li_c
        g_r = b_last - b_r + li_r
        m_new = jnp.maximum(b_last + m_prev, jnp.max(g_r, axis=-1, keepdims=True))
        decay = jnp.exp(b_last + m_prev - m_new)
        kw = k * jnp.exp(g_c - m_new)
        c_ref[st] = decay * c_prev + _dot_tn(kw.astype(BF16), vb)
        n_ref[st] = decay * n_prev + jnp.sum(kw, axis=0, keepdims=True)
        m_ref[st] = m_new

    z_prev = None
    for b in range(NB):
        z, pieces = projection(b)
        for h in range(A_HEADS):
            if z_prev is not None:
                head(h, b - 1, z_prev)
            pieces[h]()
        z_prev = z
    for h in range(A_HEADS):
        head(h, NB - 1, z_prev)

    for b in range(NB):
        out_ref[b] = x_ref[b] + _dot(hcat_ref[b], wo_ref[...])


def _mlstm_layer(x, norm_g, w_in, b_if, hnorm_g, w_out):
    B, S, D = x.shape
    L = MLSTM_CHUNK
    wg = w_in[:, A_Z_DIM:]
    ng = 2 * A_HEADS
    wz = jnp.pad(w_in, ((0, 0), (0, GATE_LANES - ng))).astype(BF16)
    wgr = jnp.pad(wg.T, ((0, GATE_ROWS - ng), (0, 0))).astype(BF16)
    bc = jnp.pad(b_if[None, :], ((0, 0), (0, GATE_LANES - ng)))
    br = jnp.pad(b_if[:, None], ((0, GATE_ROWS - ng), (0, 0)))
    row_spec = pl.BlockSpec((B, L, D), lambda c: (0, c, 0))
    return pl.pallas_call(
        _mlstm_kernel,
        out_shape=jax.ShapeDtypeStruct((B, S, D), F32),
        grid=(S // L,),
        in_specs=[row_spec, _const_spec((1, D)), _const_spec((D, A_Z_DIM + GATE_LANES)),
                  _const_spec((GATE_ROWS, D)),
                  _const_spec((1, GATE_LANES)), _const_spec((GATE_ROWS, 1)),
                  _const_spec((1, A_NV)), _const_spec((A_NV, D))],
        out_specs=row_spec,
        scratch_shapes=[pltpu.VMEM((B * A_HEADS, A_QK_DIM, A_V_DIM), F32),
                        pltpu.VMEM((B * A_HEADS, 1, A_QK_DIM), F32),
                        pltpu.VMEM((B * A_HEADS, 1, 1), F32),
                        pltpu.VMEM((B, L, A_NV), BF16)],
        compiler_params=pltpu.CompilerParams(
            dimension_semantics=("arbitrary",), vmem_limit_bytes=VMEM_LIMIT),
        name="mlstm_layer",
    )(x, norm_g[None, :], wz, wgr, bc, br, hnorm_g.reshape(1, A_NV), w_out.astype(BF16))


def _ffn_kernel(x_ref, *rest, final_layer):
    if final_layer:
        (branch_ref, g_ref, wup_ref, cw_ref, cb_ref, wdn_ref, fg_ref,
         out_ref, ubuf_ref, act_ref, slab_ref) = rest
    else:
        (g_ref, wup_ref, cw_ref, cb_ref, wdn_ref,
         out_ref, xh0_ref, xh1_ref, xh2_ref, ubuf_ref, act_ref, slab_ref, cls_ref) = rest
    T = x_ref.shape[0]
    H = CONV_HALO
    half = T // 2
    nslab = D_MODEL // LANES

    @pl.when(pl.program_id(1) == 0)
    def _():
        ubuf_ref[:, 0:H, :] = jnp.zeros((2 * D_FF // LANES, H, LANES), F32)

    x = x_ref[...]
    if final_layer:
        x = x + branch_ref[...].astype(F32)
    xn = (x * _rms_scale(x) * g_ref[...]).astype(BF16)
    per = FFN_COLS // LANES

    def up(c):
        for base in (0, D_FF):
            u = _dot(xn, wup_ref[:, base + c * FFN_COLS:base + (c + 1) * FFN_COLS])
            for s in range(per):
                ubuf_ref[base // LANES + c * per + s, H:H + T, :] = u[:, s * LANES:(s + 1) * LANES]

    def conv(k, parity):
        cs = slice(k * LANES, (k + 1) * LANES)
        taps = [ubuf_ref[k, pl.ds(H + parity - 2 + i, half, stride=2), :] for i in range(3)]
        return (taps[0] * cw_ref[0:1, cs] + taps[1] * cw_ref[1:2, cs] + taps[2] * cw_ref[2:3, cs]
                + cb_ref[:, cs])

    def activate(c):
        for j in range(c * per, (c + 1) * per):
            for parity in range(2):
                gate = conv(j, parity)
                val = conv(D_FF // LANES + j, parity)
                act_ref[parity * half:(parity + 1) * half, j * LANES:(j + 1) * LANES] = (
                    gate * jax.nn.sigmoid(gate) * val).astype(BF16)

    nchunk = D_FF // FFN_COLS
    up(0)
    for c in range(nchunk):
        if c + 1 < nchunk:
            up(c + 1)
        activate(c)

    ubuf_ref[:, 0:H, :] = ubuf_ref[:, T:T + H, :]
    down = _dot(act_ref[...], wdn_ref[...])
    for k in range(nslab):
        for parity in range(2):
            slab_ref[k, pl.ds(parity, half, stride=2), :] = (
                down[parity * half:(parity + 1) * half, k * LANES:(k + 1) * LANES])
    y = x + jnp.concatenate([slab_ref[k] for k in range(nslab)], axis=1)
    if final_layer:
        out_ref[...] = y * _rms_scale(y) * fg_ref[...]
        return
    out_ref[...] = y
    xh = y * _rms_scale(y)
    xh0_ref[...] = xh.astype(BF16)
    for k in range(nslab):
        slab_ref[k] = xh[:, k * LANES:(k + 1) * LANES]
    d1, d2 = B_GROUPS[1][1], B_GROUPS[2][1]
    sub = d2 // d1
    for r1 in range(d1):
        for k in range(nslab):
            rows = slab_ref[k, pl.ds(r1, T // d1, stride=d1), :]
            cls_ref[r1 * nslab + k] = rows
            c0 = r1 * D_MODEL + k * LANES
            xh1_ref[:, c0:c0 + LANES] = rows.astype(BF16)
    for r1 in range(d1):
        for q in range(sub):
            for k in range(nslab):
                c0 = (r1 + d1 * q) * D_MODEL + k * LANES
                xh2_ref[:, c0:c0 + LANES] = (
                    cls_ref[r1 * nslab + k, pl.ds(q, T // d2, stride=sub), :].astype(BF16))


def _conv_ffn(x, layer, norm_g, w_up, conv_w, conv_b, w_down, branch=None, final_g=None):
    B, S, D = x.shape
    T = FFN_ROWS
    final_layer = branch is not None
    row_spec = pl.BlockSpec((None, T, D), lambda b, t: (b, t, 0))

    def layer_spec(*shape):
        return pl.BlockSpec((None,) + shape, lambda *_: (layer,) + (0,) * len(shape),
                            pipeline_mode=pl.Buffered(1))

    weight_specs = [layer_spec(1, D), layer_spec(D, 2 * D_FF), layer_spec(3, 2 * D_FF),
                    layer_spec(1, 2 * D_FF), layer_spec(D_FF, D)]
    weights = [norm_g[:, None, :], w_up, conv_w, conv_b[:, None, :], w_down]
    out_shape = [jax.ShapeDtypeStruct((B, S, D), F32)]
    out_specs = [row_spec]
    scratch = [pltpu.VMEM((2 * D_FF // LANES, T + CONV_HALO, LANES), F32),
               pltpu.VMEM((T, D_FF), BF16), pltpu.VMEM((D // LANES, T, LANES), F32)]
    if final_layer:
        in_specs = [row_spec, row_spec] + weight_specs + [_const_spec((1, D))]
        operands = [x, branch] + weights + [final_g[None, :]]
    else:
        in_specs = [row_spec] + weight_specs
        operands = [x] + weights
        out_shape.append(jax.ShapeDtypeStruct((B, S, D), BF16))
        out_specs.append(row_spec)
        for _, dil in B_GROUPS[1:]:
            out_shape.append(jax.ShapeDtypeStruct((B, S // dil, dil * D), BF16))
            out_specs.append(pl.BlockSpec((None, T // dil, dil * D), lambda b, t: (b, t, 0)))
        d1 = B_GROUPS[1][1]
        scratch.append(pltpu.VMEM((d1 * D // LANES, T // d1, LANES), F32))
    res = pl.pallas_call(
        functools.partial(_ffn_kernel, final_layer=final_layer),
        out_shape=out_shape,
        grid=(B, S // T),
        in_specs=in_specs,
        out_specs=out_specs,
        scratch_shapes=scratch,
        compiler_params=pltpu.CompilerParams(
            dimension_semantics=("arbitrary", "arbitrary"), vmem_limit_bytes=VMEM_LIMIT),
        name="conv_ffn_final" if final_layer else "conv_ffn",
    )(*operands)
    return res[0] if final_layer else res


def _attn_kernel(xh_ref, wq_ref, wkt_ref, wv_ref, band_ref, o_ref, lse_ref,
                 q2_ref, kt_ref, v_ref, bias_ref):
    T = xh_ref.shape[0]
    P = B_BLOCK
    i = pl.program_id(2)

    @pl.when((pl.program_id(0) == 0) & (pl.program_id(1) == 0) & (i == 0))
    def _():
        prev_key = lax.broadcasted_iota(jnp.int32, (P, 2 * P), 1) < P
        n = band_ref.shape[1]
        for h in range(B_HEADS):
            rows = jnp.broadcast_to(band_ref[h:h + 1, :], (P, n))
            table = pltpu.roll(rows, 0, 1, stride=1, stride_axis=0)[:, :2 * P]
            half = slice((h % 2) * P, (h % 2 + 1) * P)
            bias_ref[0, h // 2, half, :] = table
            bias_ref[1, h // 2, half, :] = jnp.where(prev_key, MASK_VALUE, table)

    @pl.when(i == 0)
    def _():
        kt_ref[:, 0:P] = jnp.zeros((D_MODEL, P), BF16)
        v_ref[0:P, :] = jnp.zeros((P, D_MODEL), BF16)

    @pl.when(i > 0)
    def _():
        kt_ref[:, 0:P] = kt_ref[:, T:T + P]
        v_ref[0:P, :] = v_ref[T:T + P, :]

    lane = lax.broadcasted_iota(jnp.int32, (P, D_MODEL), 1)
    even_head = (lane % (2 * B_HEAD_DIM)) < B_HEAD_DIM
    lane2 = lax.broadcasted_iota(jnp.int32, (P, 2 * B_HEAD_DIM), 1)
    first_half = lane2 < B_HEAD_DIM
    ones = jnp.ones((2 * P, LANES), BF16)
    first = (i == 0).astype(jnp.int32)

    xh = xh_ref[...]
    q = _dot(xh, wq_ref[...]) * (B_HEAD_DIM ** -0.5)
    for j in range(T // P):
        qj = q[j * P:(j + 1) * P]
        q2_ref[j, 0:P, :] = jnp.where(even_head, qj, 0.0).astype(BF16)
        q2_ref[j, P:2 * P, :] = jnp.where(even_head, 0.0, qj).astype(BF16)
    kt_ref[:, P:P + T] = _dot_nt(wkt_ref[...], xh).astype(BF16)
    v_ref[P:P + T, :] = _dot(xh, wv_ref[...]).astype(BF16)

    for j in range(T // P):
        lse_all = jnp.zeros((P, LANES), F32)
        for hp in range(B_HEADS // 2):
            cs = slice(hp * 2 * B_HEAD_DIM, (hp + 1) * 2 * B_HEAD_DIM)
            s = (_dot(q2_ref[j, :, cs], kt_ref[cs, j * P:(j + 2) * P])
                 + bias_ref[first if j == 0 else 0, hp])
            m = jnp.max(s, axis=-1, keepdims=True)
            p = jnp.exp(s - m).astype(BF16)
            va = jnp.concatenate([v_ref[j * P:(j + 2) * P, cs], ones], axis=1)
            pv = _dot(p, va)
            l = pv[:, LANES:]
            o = pv[:, :LANES] * (1.0 / l)
            lse = m + jnp.log(l)
            o_ref[j * P:(j + 1) * P, cs] = jnp.where(first_half, o[0:P], o[P:2 * P]).astype(BF16)
            lse_all = jnp.where(lane2 == 2 * hp, lse[0:P],
                                jnp.where(lane2 == 2 * hp + 1, lse[P:2 * P], lse_all))
        lse_ref[j * P:(j + 1) * P, :] = lse_all


def _t5_bucket(dist):
    max_exact = REL_BUCKETS // 2
    d = np.maximum(dist, 0)
    log_ratio = np.log(np.maximum(d, 1) / max_exact) / math.log(REL_MAX_DIST / max_exact)
    large = np.minimum(max_exact + (log_ratio * (REL_BUCKETS - max_exact)).astype(np.int64),
                       REL_BUCKETS - 1)
    return np.where(d < max_exact, d, large).astype(np.int32)


def _band_bias(rel_bias):
    P = B_BLOCK
    n = 3 * P
    t = np.arange(n)
    delta = P - np.where(t < 2 * P, t, t - n)
    vecs = []
    for g, (win, dil) in enumerate(B_GROUPS):
        valid = (delta >= 0) & (delta <= win // dil)
        vec = rel_bias[_t5_bucket(delta * dil)][:, g * B_HEADS:(g + 1) * B_HEADS].astype(F32)
        vecs.append(jnp.where(valid[:, None], vec, MASK_VALUE).T)
    return jnp.stack(vecs)


def _attn_group(xh, g, dil, wq_all, wkt_all, wv_all, bias_all):
    B, n, _ = xh.shape
    D = D_MODEL
    T = ATT_ROWS
    cls_spec = pl.BlockSpec((None, T, D), lambda b, r, i: (b, i, r))
    lse_spec = pl.BlockSpec((None, T, LANES), lambda b, r, i: (b, i, r))

    def weight_spec(row_blk, col_blk):
        return pl.BlockSpec((D, D), lambda *_: (row_blk, col_blk), pipeline_mode=pl.Buffered(1))

    return pl.pallas_call(
        _attn_kernel,
        out_shape=[jax.ShapeDtypeStruct((B, n, dil * D), BF16),
                   jax.ShapeDtypeStruct((B, n, dil * LANES), F32)],
        grid=(B, dil, n // T),
        in_specs=[cls_spec, weight_spec(0, g), weight_spec(g, 0), weight_spec(0, g),
                  pl.BlockSpec((None,) + bias_all.shape[1:], lambda *_: (g, 0, 0),
                               pipeline_mode=pl.Buffered(1))],
        out_specs=[cls_spec, lse_spec],
        scratch_shapes=[pltpu.VMEM((T // B_BLOCK, 2 * B_BLOCK, D), BF16),
                        pltpu.VMEM((D, T + B_BLOCK), BF16), pltpu.VMEM((T + B_BLOCK, D), BF16),
                        pltpu.VMEM((2, B_HEADS // 2, 2 * B_BLOCK, 2 * B_BLOCK), F32)],
        compiler_params=pltpu.CompilerParams(
            dimension_semantics=("arbitrary", "arbitrary", "arbitrary"),
            vmem_limit_bytes=VMEM_LIMIT),
        name=f"dilated_attn_g{g}",
    )(xh, wq_all, wkt_all, wv_all, bias_all)


def _merge_kernel(o0_ref, o1_ref, o2_ref, l0_ref, l1_ref, l2_ref, ex_ref, wo_ref, out_ref,
                  os1_ref, os2_ref, ls1_ref, ls2_ref, mg_ref, mid_ref):
    T = o0_ref.shape[0]
    nslab = D_MODEL // LANES
    d1, d2 = B_GROUPS[1][1], B_GROUPS[2][1]
    sub = d2 // d1
    for dil, l_src, l_dst in ((d1, l1_ref, ls1_ref), (d2, l2_ref, ls2_ref)):
        for r in range(dil):
            l_dst[pl.ds(r, T // dil, stride=dil), :] = l_src[:, r * LANES:(r + 1) * LANES]
    l0, l1, l2 = l0_ref[...], ls1_ref[...], ls2_ref[...]
    m = jnp.maximum(jnp.maximum(l0, l1), l2)
    e0, e1, e2 = jnp.exp(l0 - m), jnp.exp(l1 - m), jnp.exp(l2 - m)
    inv = 1.0 / (e0 + e1 + e2)

    def spread(w):
        return _dot(w.astype(BF16), ex_ref[...])

    w0, w1 = spread(e0 * inv), spread(e1 * inv)

    for k in range(nslab):
        for r1 in range(d1):
            c0 = r1 * D_MODEL + k * LANES
            os1_ref[k, pl.ds(r1, T // d1, stride=d1), :] = o1_ref[:, c0:c0 + LANES].astype(F32)
            for q in range(sub):
                c0 = (r1 + d1 * q) * D_MODEL + k * LANES
                mid_ref[k * d1 + r1, pl.ds(q, T // d2, stride=sub), :] = (
                    o2_ref[:, c0:c0 + LANES].astype(F32))
            os2_ref[k, pl.ds(r1, T // d1, stride=d1), :] = mid_ref[k * d1 + r1]
        ks = slice(k * LANES, (k + 1) * LANES)
        o2 = os2_ref[k]
        mg = o2 + w0[:, ks] * (o0_ref[:, ks].astype(F32) - o2) + w1[:, ks] * (os1_ref[k] - o2)
        mg_ref[:, ks] = mg.astype(BF16)
    out_ref[...] = _dot(mg_ref[...], wo_ref[...]).astype(BF16)


def _merge_groups(outs, lses, w_out):
    B, S, D = outs[0].shape
    T = MERGE_ROWS
    row_spec = pl.BlockSpec((None, T, D), lambda b, t: (b, t, 0))
    o_specs = [pl.BlockSpec((None, T // dil, dil * D), lambda b, t: (b, t, 0)) for _, dil in B_GROUPS]
    l_specs = [pl.BlockSpec((None, T // dil, dil * LANES), lambda b, t: (b, t, 0))
               for _, dil in B_GROUPS]
    expand = np.zeros((LANES, D), np.float32)
    for h in range(B_HEADS):
        expand[h, h * B_HEAD_DIM:(h + 1) * B_HEAD_DIM] = 1.0
    nslab = D // LANES
    return pl.pallas_call(
        _merge_kernel,
        out_shape=jax.ShapeDtypeStruct((B, S, D), BF16),
        grid=(B, S // T),
        in_specs=o_specs + l_specs + [_const_spec((LANES, D)), _const_spec((D, D))],
        out_specs=row_spec,
        scratch_shapes=[pltpu.VMEM((nslab, T, LANES), F32), pltpu.VMEM((nslab, T, LANES), F32),
                        pltpu.VMEM((T, LANES), F32), pltpu.VMEM((T, LANES), F32),
                        pltpu.VMEM((T, D), BF16),
                        pltpu.VMEM((nslab * B_GROUPS[1][1], T // B_GROUPS[1][1], LANES), F32)],
        compiler_params=pltpu.CompilerParams(
            dimension_semantics=("arbitrary", "arbitrary"), vmem_limit_bytes=VMEM_LIMIT),
        name="merge_groups",
    )(*outs, *lses, jnp.asarray(expand, BF16), w_out.astype(BF16))


def kernel(x, a_norm_g, a_w_in, a_b_if, a_hnorm_g, a_w_out, kv_norm_g, w_kv, b_norm_g, b_w_q,
           b_w_out, rel_bias, f_norm_g, f_w_up, f_conv_w, f_conv_b, f_w_down, final_norm_g):
    x = _mlstm_layer(x, a_norm_g[0], a_w_in[0], a_b_if[0], a_hnorm_g[0], a_w_out[0])
    ffn_params = (f_norm_g, f_w_up.astype(BF16), f_conv_w, f_conv_b, f_w_down.astype(BF16))
    x, *streams = _conv_ffn(x, 0, *ffn_params)
    att_w = len(B_GROUPS) * B_HEADS * B_HEAD_DIM
    wq_all = (b_norm_g[0][:, None] * b_w_q[0]).astype(BF16)
    wkt_all = (kv_norm_g[:, None] * w_kv[:, :att_w]).T.astype(BF16)
    wv_all = (kv_norm_g[:, None] * w_kv[:, att_w:]).astype(BF16)
    bias_all = _band_bias(rel_bias)
    outs, lses = [], []
    for g, (_, dil) in enumerate(B_GROUPS):
        o, lse = _attn_group(streams[g], g, dil, wq_all, wkt_all, wv_all, bias_all)
        outs.append(o)
        lses.append(lse)
    branch = _merge_groups(outs, lses, b_w_out[0])
    return _conv_ffn(x, 1, *ffn_params, branch=branch, final_g=final_norm_g)
```

```python
import functools
import math

import numpy as np
import jax
import jax.numpy as jnp
from jax import lax
from jax.experimental import pallas as pl
from jax.experimental.pallas import tpu as pltpu

F32 = jnp.float32
BF16 = jnp.bfloat16

D_MODEL = 1024
A_HEADS = 4
A_QK_DIM = 128
A_V_DIM = 256
A_NQ = A_HEADS * A_QK_DIM
A_NV = A_HEADS * A_V_DIM
A_Z_DIM = 2 * A_NQ + 2 * A_NV
GATE_SOFTCAP = 15.0
B_GROUPS = ((128, 1), (512, 4), (2048, 16))
B_HEAD_DIM = 64
B_HEADS = 16
B_BLOCK = 128
REL_BUCKETS = 32
REL_MAX_DIST = 2048
D_FF = 2816
EPS = 1e-6
MASK_VALUE = -1e30

LANES = 128
SUBLANES = 8
BF16_ROWS = 2 * SUBLANES
MXU_TILE = 256
V7X_VMEM_BYTES = 64 * 1024 * 1024

GATE_LANES = LANES
GATE_ROWS = BF16_ROWS
MLSTM_CHUNK = 256
FFN_ROWS = 512
FFN_COLS = MXU_TILE
CONV_HALO = SUBLANES
ATT_ROWS = 1024
MERGE_ROWS = 1024
VMEM_LIMIT = V7X_VMEM_BYTES * 7 // 8


def _rms_scale(x):
    return lax.rsqrt(jnp.mean(x * x, axis=-1, keepdims=True) + EPS)


def _softcap(z):
    return GATE_SOFTCAP * jnp.tanh(z / GATE_SOFTCAP)


def _log_sigmoid(a):
    return jnp.minimum(a, 0.0) - jnp.log1p(jnp.exp(-jnp.abs(a)))


def _split3(v):
    hi = v.astype(BF16)
    r1 = v - hi.astype(F32)
    mid = r1.astype(BF16)
    lo = (r1 - mid.astype(F32)).astype(BF16)
    return hi, mid, lo


def _dot(a, b):
    return jnp.dot(a, b, preferred_element_type=F32)


def _dot_nt(a, b):
    return lax.dot_general(a, b, (((1,), (1,)), ((), ())), preferred_element_type=F32)


def _dot_tn(a, b):
    return lax.dot_general(a, b, (((0,), (0,)), ((), ())), preferred_element_type=F32)


def _const_spec(shape):
    return pl.BlockSpec(shape, lambda *_: (0,) * len(shape), pipeline_mode=pl.Buffered(1))


def _mlstm_kernel(x_ref, g_ref, wz_ref, wgr_ref, bc_ref, br_ref, hg_ref, wo_ref,
                  out_ref, c_ref, n_ref, m_ref, hcat_ref):
    NB, L = x_ref.shape[0], x_ref.shape[1]

    @pl.when(pl.program_id(0) == 0)
    def _():
        c_ref[...] = jnp.zeros_like(c_ref)
        n_ref[...] = jnp.zeros_like(n_ref)
        m_ref[...] = jnp.zeros_like(m_ref)

    row = lax.broadcasted_iota(jnp.int32, (L, L), 0)
    col = lax.broadcasted_iota(jnp.int32, (L, L), 1)
    causal = col <= row
    tril = jnp.where(causal, 1.0, 0.0).astype(BF16)
    triu = jnp.where(row <= col, 1.0, 0.0).astype(BF16)

    def projection(b):
        x = x_ref[b]
        xn = (x * _rms_scale(x) * g_ref[...]).astype(BF16)
        z = {}

        def gates():
            z["ac"] = _softcap(_dot(xn, wz_ref[:, A_Z_DIM:]) + bc_ref[...])
            z["ar"] = _softcap(_dot_nt(wgr_ref[...], xn) + br_ref[...])
            z["bcs"] = sum(_dot(tril, p) for p in _split3(_log_sigmoid(z["ac"])))
            z["brs"] = sum(_dot(p, triu) for p in _split3(_log_sigmoid(z["ar"])))

        def columns(name, c0, c1):
            z[name] = _dot(xn, wz_ref[:, c0:c1])

        return z, [gates,
                   functools.partial(columns, "qk", 0, 2 * A_NQ),
                   functools.partial(columns, "v", 2 * A_NQ, 2 * A_NQ + A_NV),
                   functools.partial(columns, "o", 2 * A_NQ + A_NV, A_Z_DIM)]

    def head(h, b, z):
        st = b * A_HEADS + h
        q = z["qk"][:, h * A_QK_DIM:(h + 1) * A_QK_DIM] * (A_QK_DIM ** -0.5)
        k = z["qk"][:, A_NQ + h * A_QK_DIM:A_NQ + (h + 1) * A_QK_DIM]
        v = z["v"][:, h * A_V_DIM:(h + 1) * A_V_DIM]
        o = z["o"][:, h * A_V_DIM:(h + 1) * A_V_DIM]
        qb, kb, vb = q.astype(BF16), k.astype(BF16), v.astype(BF16)
        li_c = z["ac"][:, h:h + 1]
        li_r = z["ar"][h:h + 1, :]
        b_c = z["bcs"][:, A_HEADS + h:A_HEADS + h + 1]
        b_r = z["brs"][A_HEADS + h:A_HEADS + h + 1, :]
        m_prev = m_ref[st]
        c_prev = c_ref[st]
        n_prev = n_ref[st]

        log_d = jnp.where(causal, b_c - b_r + li_r, -jnp.inf)
        m_inter = b_c + m_prev
        m_t = jnp.maximum(m_inter, jnp.max(log_d, axis=-1, keepdims=True))
        sm = _dot_nt(qb, kb) * jnp.exp(log_d - m_t)
        w_inter = jnp.exp(m_inter - m_t)
        num = _dot(sm.astype(BF16), vb) + w_inter * _dot(qb, c_prev.astype(BF16))
        den = (jnp.sum(sm, axis=-1, keepdims=True)
               + w_inter * jnp.sum(q * n_prev, axis=-1, keepdims=True))
        hv = num * (1.0 / jnp.maximum(jnp.abs(den), jnp.exp(-m_t)))
        hv = hv * _rms_scale(hv) * hg_ref[:, h * A_V_DIM:(h + 1) * A_V_DIM]
        hcat_ref[b, :, h * A_V_DIM:(h + 1) * A_V_DIM] = (hv * jax.nn.sigmoid(o)).astype(BF16)

        b_last = b_c[L - 1:L, :]
        g_c = b_last - b_c + li_c
        g_r = b_last - b_r + li_r
        m_new = jnp.maximum(b_last + m_prev, jnp.max(g_r, axis=-1, keepdims=True))
        decay = jnp.exp(b_last + m_prev - m_new)
        kw = k * jnp.exp(g_c - m_new)
        c_ref[st] = decay * c_prev + _dot_tn(kw.astype(BF16), vb)
        n_ref[st] = decay * n_prev + jnp.sum(kw, axis=0, keepdims=True)
        m_ref[st] = m_new

    z_prev = None
    for b in range(NB):
        z, pieces = projection(b)
        for h in range(A_HEADS):
            if z_prev is not None:
                head(h, b - 1, z_prev)
            pieces[h]()
        z_prev = z
    for h in range(A_HEADS):
        head(h, NB - 1, z_prev)

    for b in range(NB):
        out_ref[b] = x_ref[b] + _dot(hcat_ref[b], wo_ref[...])


def _mlstm_layer(x, norm_g, w_in, b_if, hnorm_g, w_out):
    B, S, D = x.shape
    L = MLSTM_CHUNK
    wg = w_in[:, A_Z_DIM:]
    ng = 2 * A_HEADS
    wz = jnp.pad(w_in, ((0, 0), (0, GATE_LANES - ng))).astype(BF16)
    wgr = jnp.pad(wg.T, ((0, GATE_ROWS - ng), (0, 0))).astype(BF16)
    bc = jnp.pad(b_if[None, :], ((0, 0), (0, GATE_LANES - ng)))
    br = jnp.pad(b_if[:, None], ((0, GATE_ROWS - ng), (0, 0)))
    row_spec = pl.BlockSpec((B, L, D), lambda c: (0, c, 0))
    return pl.pallas_call(
        _mlstm_kernel,
        out_shape=jax.ShapeDtypeStruct((B, S, D), F32),
        grid=(S // L,),
        in_specs=[row_spec, _const_spec((1, D)), _const_spec((D, A_Z_DIM + GATE_LANES)),
                  _const_spec((GATE_ROWS, D)),
                  _const_spec((1, GATE_LANES)), _const_spec((GATE_ROWS, 1)),
                  _const_spec((1, A_NV)), _const_spec((A_NV, D))],
        out_specs=row_spec,
        scratch_shapes=[pltpu.VMEM((B * A_HEADS, A_QK_DIM, A_V_DIM), F32),
                        pltpu.VMEM((B * A_HEADS, 1, A_QK_DIM), F32),
                        pltpu.VMEM((B * A_HEADS, 1, 1), F32),
                        pltpu.VMEM((B, L, A_NV), BF16)],
        compiler_params=pltpu.CompilerParams(
            dimension_semantics=("arbitrary",), vmem_limit_bytes=VMEM_LIMIT),
        name="mlstm_layer",
    )(x, norm_g[None, :], wz, wgr, bc, br, hnorm_g.reshape(1, A_NV), w_out.astype(BF16))


def _ffn_kernel(x_ref, *rest, final_layer):
    if final_layer:
        (branch_ref, g_ref, wup_ref, cw_ref, cb_ref, wdn_ref, fg_ref,
         out_ref, ubuf_ref, act_ref, slab_ref) = rest
    else:
        (g_ref, wup_ref, cw_ref, cb_ref, wdn_ref,
         out_ref, xh0_ref, xh1_ref, xh2_ref, ubuf_ref, act_ref, slab_ref, cls_ref) = rest
    T = x_ref.shape[0]
    H = CONV_HALO
    half = T // 2
    nslab = D_MODEL // LANES

    @pl.when(pl.program_id(1) == 0)
    def _():
        ubuf_ref[:, 0:H, :] = jnp.zeros((2 * D_FF // LANES, H, LANES), F32)

    x = x_ref[...]
    if final_layer:
        x = x + branch_ref[...].astype(F32)
    xn = (x * _rms_scale(x) * g_ref[...]).astype(BF16)
    per = FFN_COLS // LANES

    def up(c):
        for base in (0, D_FF):
            u = _dot(xn, wup_ref[:, base + c * FFN_COLS:base + (c + 1) * FFN_COLS])
            for s in range(per):
                ubuf_ref[base // LANES + c * per + s, H:H + T, :] = u[:, s * LANES:(s + 1) * LANES]

    def conv(k, parity):
        cs = slice(k * LANES, (k + 1) * LANES)
        taps = [ubuf_ref[k, pl.ds(H + parity - 2 + i, half, stride=2), :] for i in range(3)]
        return (taps[0] * cw_ref[0:1, cs] + taps[1] * cw_ref[1:2, cs] + taps[2] * cw_ref[2:3, cs]
                + cb_ref[:, cs])

    def activate(c):
        for j in range(c * per, (c + 1) * per):
            for parity in range(2):
                gate = conv(j, parity)
                val = conv(D_FF // LANES + j, parity)
                act_ref[parity * half:(parity + 1) * half, j * LANES:(j + 1) * LANES] = (
                    gate * jax.nn.sigmoid(gate) * val).astype(BF16)

    nchunk = D_FF // FFN_COLS
    up(0)
    for c in range(nchunk):
        if c + 1 < nchunk:
            up(c + 1)
        activate(c)

    ubuf_ref[:, 0:H, :] = ubuf_ref[:, T:T + H, :]
    down = _dot(act_ref[...], wdn_ref[...])
    for k in range(nslab):
        for parity in range(2):
            slab_ref[k, pl.ds(parity, half, stride=2), :] = (
                down[parity * half:(parity + 1) * half, k * LANES:(k + 1) * LANES])
    y = x + jnp.concatenate([slab_ref[k] for k in range(nslab)], axis=1)
    if final_layer:
        out_ref[...] = y * _rms_scale(y) * fg_ref[...]
        return
    out_ref[...] = y
    xh = y * _rms_scale(y)
    xh0_ref[...] = xh.astype(BF16)
    for k in range(nslab):
        slab_ref[k] = xh[:, k * LANES:(k + 1) * LANES]
    d1, d2 = B_GROUPS[1][1], B_GROUPS[2][1]
    sub = d2 // d1
    for r1 in range(d1):
        for k in range(nslab):
            rows = slab_ref[k, pl.ds(r1, T // d1, stride=d1), :]
            cls_ref[r1 * nslab + k] = rows
            c0 = r1 * D_MODEL + k * LANES
            xh1_ref[:, c0:c0 + LANES] = rows.astype(BF16)
    for r1 in range(d1):
        for q in range(sub):
            for k in range(nslab):
                c0 = (r1 + d1 * q) * D_MODEL + k * LANES
                xh2_ref[:, c0:c0 + LANES] = (
                    cls_ref[r1 * nslab + k, pl.ds(q, T // d2, stride=sub), :].astype(BF16))


def _conv_ffn(x, layer, norm_g, w_up, conv_w, conv_b, w_down, branch=None, final_g=None):
    B, S, D = x.shape
    T = FFN_ROWS
    final_layer = branch is not None
    row_spec = pl.BlockSpec((None, T, D), lambda b, t: (b, t, 0))

    def layer_spec(*shape):
        return pl.BlockSpec((None,) + shape, lambda *_: (layer,) + (0,) * len(shape),
                            pipeline_mode=pl.Buffered(1))

    weight_specs = [layer_spec(1, D), layer_spec(D, 2 * D_FF), layer_spec(3, 2 * D_FF),
                    layer_spec(1, 2 * D_FF), layer_spec(D_FF, D)]
    weights = [norm_g[:, None, :], w_up, conv_w, conv_b[:, None, :], w_down]
    out_shape = [jax.ShapeDtypeStruct((B, S, D), F32)]
    out_specs = [row_spec]
    scratch = [pltpu.VMEM((2 * D_FF // LANES, T + CONV_HALO, LANES), F32),
               pltpu.VMEM((T, D_FF), BF16), pltpu.VMEM((D // LANES, T, LANES), F32)]
    if final_layer:
        in_specs = [row_spec, row_spec] + weight_specs + [_const_spec((1, D))]
        operands = [x, branch] + weights + [final_g[None, :]]
    else:
        in_specs = [row_spec] + weight_specs
        operands = [x] + weights
        out_shape.append(jax.ShapeDtypeStruct((B, S, D), BF16))
        out_specs.append(row_spec)
        for _, dil in B_GROUPS[1:]:
            out_shape.append(jax.ShapeDtypeStruct((B, S // dil, dil * D), BF16))
            out_specs.append(pl.BlockSpec((None, T // dil, dil * D), lambda b, t: (b, t, 0)))
        d1 = B_GROUPS[1][1]
        scratch.append(pltpu.VMEM((d1 * D // LANES, T // d1, LANES), F32))
    res = pl.pallas_call(
        functools.partial(_ffn_kernel, final_layer=final_layer),
        out_shape=out_shape,
        grid=(B, S // T),
        in_specs=in_specs,
        out_specs=out_specs,
        scratch_shapes=scratch,
        compiler_params=pltpu.CompilerParams(
            dimension_semantics=("arbitrary", "arbitrary"), vmem_limit_bytes=VMEM_LIMIT),
        name="conv_ffn_final" if final_layer else "conv_ffn",
    )(*operands)
    return res[0] if final_layer else res


def _attn_kernel(xh_ref, wq_raw_ref, wk_raw_ref, wv_raw_ref, gq_ref, gkv_ref, gkv_row_ref, band_ref,
                 o_ref, lse_ref, q2_ref, kt_ref, v_ref, bias_ref, wq_ref, wkt_ref, wv_ref):
    T = xh_ref.shape[0]
    P = B_BLOCK
    i = pl.program_id(2)

    @pl.when((pl.program_id(0) == 0) & (pl.program_id(1) == 0) & (i == 0))
    def _():
        for c in range(D_MODEL // MXU_TILE):
            rs = slice(c * MXU_TILE, (c + 1) * MXU_TILE)
            wq_ref[rs, :] = (wq_raw_ref[rs, :] * gq_ref[rs, :]).astype(BF16)
            wv_ref[rs, :] = (wv_raw_ref[rs, :] * gkv_ref[rs, :]).astype(BF16)
            wkt_ref[:, rs] = (wk_raw_ref[rs, :].T * gkv_row_ref[:, rs]).astype(BF16)
        prev_key = lax.broadcasted_iota(jnp.int32, (P, 2 * P), 1) < P
        n = band_ref.shape[1]
        for h in range(B_HEADS):
            rows = jnp.broadcast_to(band_ref[h:h + 1, :], (P, n))
            table = pltpu.roll(rows, 0, 1, stride=1, stride_axis=0)[:, :2 * P]
            half = slice((h % 2) * P, (h % 2 + 1) * P)
            bias_ref[0, h // 2, half, :] = table
            bias_ref[1, h // 2, half, :] = jnp.where(prev_key, MASK_VALUE, table)

    @pl.when(i == 0)
    def _():
        kt_ref[:, 0:P] = jnp.zeros((D_MODEL, P), BF16)
        v_ref[0:P, :] = jnp.zeros((P, D_MODEL), BF16)

    @pl.when(i > 0)
    def _():
        kt_ref[:, 0:P] = kt_ref[:, T:T + P]
        v_ref[0:P, :] = v_ref[T:T + P, :]

    lane = lax.broadcasted_iota(jnp.int32, (P, D_MODEL), 1)
    even_head = (lane % (2 * B_HEAD_DIM)) < B_HEAD_DIM
    lane2 = lax.broadcasted_iota(jnp.int32, (P, 2 * B_HEAD_DIM), 1)
    first_half = lane2 < B_HEAD_DIM
    ones = jnp.ones((2 * P, LANES), BF16)
    first = (i == 0).astype(jnp.int32)

    xh = xh_ref[...]
    q = _dot(xh, wq_ref[...]) * (B_HEAD_DIM ** -0.5)
    for j in range(T // P):
        qj = q[j * P:(j + 1) * P]
        q2_ref[j, 0:P, :] = jnp.where(even_head, qj, 0.0).astype(BF16)
        q2_ref[j, P:2 * P, :] = jnp.where(even_head, 0.0, qj).astype(BF16)
    kt_ref[:, P:P + T] = _dot_nt(wkt_ref[...], xh).astype(BF16)
    v_ref[P:P + T, :] = _dot(xh, wv_ref[...]).astype(BF16)

    for j in range(T // P):
        lse_all = jnp.zeros((P, LANES), F32)
        for hp in range(B_HEADS // 2):
            cs = slice(hp * 2 * B_HEAD_DIM, (hp + 1) * 2 * B_HEAD_DIM)
            s = (_dot(q2_ref[j, :, cs], kt_ref[cs, j * P:(j + 2) * P])
                 + bias_ref[first if j == 0 else 0, hp])
            m = jnp.max(s, axis=-1, keepdims=True)
            p = jnp.exp(s - m).astype(BF16)
            va = jnp.concatenate([v_ref[j * P:(j + 2) * P, cs], ones], axis=1)
            pv = _dot(p, va)
            l = pv[:, LANES:]
            o = pv[:, :LANES] * (1.0 / l)
            lse = m + jnp.log(l)
            o_ref[j * P:(j + 1) * P, cs] = jnp.where(first_half, o[0:P], o[P:2 * P]).astype(BF16)
            lse_all = jnp.where(lane2 == 2 * hp, lse[0:P],
                                jnp.where(lane2 == 2 * hp + 1, lse[P:2 * P], lse_all))
        lse_ref[j * P:(j + 1) * P, :] = lse_all


def _t5_bucket(dist):
    max_exact = REL_BUCKETS // 2
    d = np.maximum(dist, 0)
    log_ratio = np.log(np.maximum(d, 1) / max_exact) / math.log(REL_MAX_DIST / max_exact)
    large = np.minimum(max_exact + (log_ratio * (REL_BUCKETS - max_exact)).astype(np.int64),
                       REL_BUCKETS - 1)
    return np.where(d < max_exact, d, large).astype(np.int32)


def _band_bias(rel_bias):
    P = B_BLOCK
    n = 3 * P
    t = np.arange(n)
    delta = P - np.where(t < 2 * P, t, t - n)
    vecs = []
    for g, (win, dil) in enumerate(B_GROUPS):
        valid = (delta >= 0) & (delta <= win // dil)
        vec = rel_bias[_t5_bucket(delta * dil)][:, g * B_HEADS:(g + 1) * B_HEADS].astype(F32)
        vecs.append(jnp.where(valid[:, None], vec, MASK_VALUE).T)
    return jnp.stack(vecs)


def _attn_group(xh, g, dil, w_q, w_kv, q_gain, kv_gain, bias_all):
    B, n, _ = xh.shape
    D = D_MODEL
    T = ATT_ROWS
    cls_spec = pl.BlockSpec((None, T, D), lambda b, r, i: (b, i, r))
    lse_spec = pl.BlockSpec((None, T, LANES), lambda b, r, i: (b, i, r))

    def weight_spec(row_blk, col_blk):
        return pl.BlockSpec((D, D), lambda *_: (row_blk, col_blk), pipeline_mode=pl.Buffered(1))

    return pl.pallas_call(
        _attn_kernel,
        out_shape=[jax.ShapeDtypeStruct((B, n, dil * D), BF16),
                   jax.ShapeDtypeStruct((B, n, dil * LANES), F32)],
        grid=(B, dil, n // T),
        in_specs=[cls_spec, weight_spec(0, g), weight_spec(0, g), weight_spec(0, len(B_GROUPS) + g),
                  _const_spec((D, 1)), _const_spec((D, 1)), _const_spec((1, D)),
                  pl.BlockSpec((None,) + bias_all.shape[1:], lambda *_: (g, 0, 0),
                               pipeline_mode=pl.Buffered(1))],
        out_specs=[cls_spec, lse_spec],
        scratch_shapes=[pltpu.VMEM((T // B_BLOCK, 2 * B_BLOCK, D), BF16),
                        pltpu.VMEM((D, T + B_BLOCK), BF16), pltpu.VMEM((T + B_BLOCK, D), BF16),
                        pltpu.VMEM((2, B_HEADS // 2, 2 * B_BLOCK, 2 * B_BLOCK), F32),
                        pltpu.VMEM((D, D), BF16), pltpu.VMEM((D, D), BF16), pltpu.VMEM((D, D), BF16)],
        compiler_params=pltpu.CompilerParams(
            dimension_semantics=("arbitrary", "arbitrary", "arbitrary"),
            vmem_limit_bytes=VMEM_LIMIT),
        name=f"dilated_attn_g{g}",
    )(xh, w_q, w_kv, w_kv, q_gain[:, None], kv_gain[:, None], kv_gain[None, :], bias_all)


def _merge_kernel(o0_ref, o1_ref, o2_ref, l0_ref, l1_ref, l2_ref, ex_ref, wo_ref, out_ref,
                  os1_ref, os2_ref, ls1_ref, ls2_ref, mg_ref, mid_ref):
    T = o0_ref.shape[0]
    nslab = D_MODEL // LANES
    d1, d2 = B_GROUPS[1][1], B_GROUPS[2][1]
    sub = d2 // d1
    for dil, l_src, l_dst in ((d1, l1_ref, ls1_ref), (d2, l2_ref, ls2_ref)):
        for r in range(dil):
            l_dst[pl.ds(r, T // dil, stride=dil), :] = l_src[:, r * LANES:(r + 1) * LANES]
    l0, l1, l2 = l0_ref[...], ls1_ref[...], ls2_ref[...]
    m = jnp.maximum(jnp.maximum(l0, l1), l2)
    e0, e1, e2 = jnp.exp(l0 - m), jnp.exp(l1 - m), jnp.exp(l2 - m)
    inv = 1.0 / (e0 + e1 + e2)

    def spread(w):
        return _dot(w.astype(BF16), ex_ref[...])

    w0, w1 = spread(e0 * inv), spread(e1 * inv)

    for k in range(nslab):
        for r1 in range(d1):
            c0 = r1 * D_MODEL + k * LANES
            os1_ref[k, pl.ds(r1, T // d1, stride=d1), :] = o1_ref[:, c0:c0 + LANES].astype(F32)
            for q in range(sub):
                c0 = (r1 + d1 * q) * D_MODEL + k * LANES
                mid_ref[k * d1 + r1, pl.ds(q, T // d2, stride=sub), :] = (
                    o2_ref[:, c0:c0 + LANES].astype(F32))
            os2_ref[k, pl.ds(r1, T // d1, stride=d1), :] = mid_ref[k * d1 + r1]
        ks = slice(k * LANES, (k + 1) * LANES)
        o2 = os2_ref[k]
        mg = o2 + w0[:, ks] * (o0_ref[:, ks].astype(F32) - o2) + w1[:, ks] * (os1_ref[k] - o2)
        mg_ref[:, ks] = mg.astype(BF16)
    out_ref[...] = _dot(mg_ref[...], wo_ref[...]).astype(BF16)


def _merge_groups(outs, lses, w_out):
    B, S, D = outs[0].shape
    T = MERGE_ROWS
    row_spec = pl.BlockSpec((None, T, D), lambda b, t: (b, t, 0))
    o_specs = [pl.BlockSpec((None, T // dil, dil * D), lambda b, t: (b, t, 0)) for _, dil in B_GROUPS]
    l_specs = [pl.BlockSpec((None, T // dil, dil * LANES), lambda b, t: (b, t, 0))
               for _, dil in B_GROUPS]
    expand = np.zeros((LANES, D), np.float32)
    for h in range(B_HEADS):
        expand[h, h * B_HEAD_DIM:(h + 1) * B_HEAD_DIM] = 1.0
    nslab = D // LANES
    return pl.pallas_call(
        _merge_kernel,
        out_shape=jax.ShapeDtypeStruct((B, S, D), BF16),
        grid=(B, S // T),
        in_specs=o_specs + l_specs + [_const_spec((LANES, D)), _const_spec((D, D))],
        out_specs=row_spec,
        scratch_shapes=[pltpu.VMEM((nslab, T, LANES), F32), pltpu.VMEM((nslab, T, LANES), F32),
                        pltpu.VMEM((T, LANES), F32), pltpu.VMEM((T, LANES), F32),
                        pltpu.VMEM((T, D), BF16),
                        pltpu.VMEM((nslab * B_GROUPS[1][1], T // B_GROUPS[1][1], LANES), F32)],
        compiler_params=pltpu.CompilerParams(
            dimension_semantics=("arbitrary", "arbitrary"), vmem_limit_bytes=VMEM_LIMIT),
        name="merge_groups",
    )(*outs, *lses, jnp.asarray(expand, BF16), w_out.astype(BF16))


def kernel(x, a_norm_g, a_w_in, a_b_if, a_hnorm_g, a_w_out, kv_norm_g, w_kv, b_norm_g, b_w_q,
           b_w_out, rel_bias, f_norm_g, f_w_up, f_conv_w, f_conv_b, f_w_down, final_norm_g):
    x = _mlstm_layer(x, a_norm_g[0], a_w_in[0], a_b_if[0], a_hnorm_g[0], a_w_out[0])
    ffn_params = (f_norm_g, f_w_up.astype(BF16), f_conv_w, f_conv_b, f_w_down.astype(BF16))
    x, *streams = _conv_ffn(x, 0, *ffn_params)
    bias_all = _band_bias(rel_bias)
    outs, lses = [], []
    for g, (_, dil) in enumerate(B_GROUPS):
        o, lse = _attn_group(streams[g], g, dil, b_w_q[0], w_kv, b_norm_g[0], kv_norm_g, bias_all)
        outs.append(o)
        lses.append(lse)
    branch = _merge_groups(outs, lses, b_w_out[0])
    return _conv_ffn(x, 1, *ffn_params, branch=branch, final_g=final_norm_g)
```

```python
import functools
import math

import numpy as np
import jax
import jax.numpy as jnp
from jax import lax
from jax.experimental import pallas as pl
from jax.experimental.pallas import tpu as pltpu

F32 = jnp.float32
BF16 = jnp.bfloat16

D_MODEL = 1024
A_HEADS = 4
A_QK_DIM = 128
A_V_DIM = 256
A_NQ = A_HEADS * A_QK_DIM
A_NV = A_HEADS * A_V_DIM
A_Z_DIM = 2 * A_NQ + 2 * A_NV
GATE_SOFTCAP = 15.0
B_GROUPS = ((128, 1), (512, 4), (2048, 16))
B_HEAD_DIM = 64
B_HEADS = 16
B_BLOCK = 128
REL_BUCKETS = 32
REL_MAX_DIST = 2048
D_FF = 2816
EPS = 1e-6
MASK_VALUE = -1e30

LANES = 128
SUBLANES = 8
BF16_ROWS = 2 * SUBLANES
MXU_TILE = 256
V7X_VMEM_BYTES = 64 * 1024 * 1024

GATE_LANES = LANES
GATE_ROWS = BF16_ROWS
MLSTM_CHUNK = 256
FFN_ROWS = 512
FFN_COLS = MXU_TILE
CONV_HALO = SUBLANES
ATT_ROWS = 1024
MERGE_ROWS = 1024
VMEM_LIMIT = V7X_VMEM_BYTES * 7 // 8


def _rms_scale(x):
    return lax.rsqrt(jnp.mean(x * x, axis=-1, keepdims=True) + EPS)


def _softcap(z):
    return GATE_SOFTCAP * jnp.tanh(z / GATE_SOFTCAP)


def _log_sigmoid(a):
    return jnp.minimum(a, 0.0) - jnp.log1p(jnp.exp(-jnp.abs(a)))


def _split3(v):
    hi = v.astype(BF16)
    r1 = v - hi.astype(F32)
    mid = r1.astype(BF16)
    lo = (r1 - mid.astype(F32)).astype(BF16)
    return hi, mid, lo


def _dot(a, b):
    return jnp.dot(a, b, preferred_element_type=F32)


def _dot_nt(a, b):
    return lax.dot_general(a, b, (((1,), (1,)), ((), ())), preferred_element_type=F32)


def _dot_tn(a, b):
    return lax.dot_general(a, b, (((0,), (0,)), ((), ())), preferred_element_type=F32)


def _const_spec(shape):
    return pl.BlockSpec(shape, lambda *_: (0,) * len(shape), pipeline_mode=pl.Buffered(1))


def _mlstm_kernel(x_ref, g_ref, win_ref, wgc_ref, wgr_ref, bc_ref, br_ref, hg_ref, wout_ref,
                  out_ref, c_ref, n_ref, m_ref, hcat_ref, wz_ref, wo_ref):
    NB, L = x_ref.shape[0], x_ref.shape[1]

    @pl.when(pl.program_id(0) == 0)
    def _():
        c_ref[...] = jnp.zeros_like(c_ref)
        n_ref[...] = jnp.zeros_like(n_ref)
        m_ref[...] = jnp.zeros_like(m_ref)
        for c in range(D_MODEL // MXU_TILE):
            rs = slice(c * MXU_TILE, (c + 1) * MXU_TILE)
            wz_ref[rs, :] = win_ref[rs, 0:A_Z_DIM].astype(BF16)
            wo_ref[rs, :] = wout_ref[rs, :].astype(BF16)

    row = lax.broadcasted_iota(jnp.int32, (L, L), 0)
    col = lax.broadcasted_iota(jnp.int32, (L, L), 1)
    causal = col <= row
    tril = jnp.where(causal, 1.0, 0.0).astype(BF16)
    triu = jnp.where(row <= col, 1.0, 0.0).astype(BF16)

    def projection(b):
        x = x_ref[b]
        xn = (x * _rms_scale(x) * g_ref[...]).astype(BF16)
        z = {}

        def gates():
            z["ac"] = _softcap(_dot(xn, wgc_ref[...]) + bc_ref[...])
            z["ar"] = _softcap(_dot_nt(wgr_ref[...], xn) + br_ref[...])
            z["bcs"] = sum(_dot(tril, p) for p in _split3(_log_sigmoid(z["ac"])))
            z["brs"] = sum(_dot(p, triu) for p in _split3(_log_sigmoid(z["ar"])))

        def columns(name, c0, c1):
            z[name] = _dot(xn, wz_ref[:, c0:c1])

        return z, [gates,
                   functools.partial(columns, "qk", 0, 2 * A_NQ),
                   functools.partial(columns, "v", 2 * A_NQ, 2 * A_NQ + A_NV),
                   functools.partial(columns, "o", 2 * A_NQ + A_NV, A_Z_DIM)]

    def head(h, b, z):
        st = b * A_HEADS + h
        q = z["qk"][:, h * A_QK_DIM:(h + 1) * A_QK_DIM] * (A_QK_DIM ** -0.5)
        k = z["qk"][:, A_NQ + h * A_QK_DIM:A_NQ + (h + 1) * A_QK_DIM]
        v = z["v"][:, h * A_V_DIM:(h + 1) * A_V_DIM]
        o = z["o"][:, h * A_V_DIM:(h + 1) * A_V_DIM]
        qb, kb, vb = q.astype(BF16), k.astype(BF16), v.astype(BF16)
        li_c = z["ac"][:, h:h + 1]
        li_r = z["ar"][h:h + 1, :]
        b_c = z["bcs"][:, A_HEADS + h:A_HEADS + h + 1]
        b_r = z["brs"][A_HEADS + h:A_HEADS + h + 1, :]
        m_prev = m_ref[st]
        c_prev = c_ref[st]
        n_prev = n_ref[st]

        log_d = jnp.where(causal, b_c - b_r + li_r, -jnp.inf)
        m_inter = b_c + m_prev
        m_t = jnp.maximum(m_inter, jnp.max(log_d, axis=-1, keepdims=True))
        sm = _dot_nt(qb, kb) * jnp.exp(log_d - m_t)
        w_inter = jnp.exp(m_inter - m_t)
        num = _dot(sm.astype(BF16), vb) + w_inter * _dot(qb, c_prev.astype(BF16))
        den = (jnp.sum(sm, axis=-1, keepdims=True)
               + w_inter * jnp.sum(q * n_prev, axis=-1, keepdims=True))
        hv = num * (1.0 / jnp.maximum(jnp.abs(den), jnp.exp(-m_t)))
        hv = hv * _rms_scale(hv) * hg_ref[:, h * A_V_DIM:(h + 1) * A_V_DIM]
        hcat_ref[b, :, h * A_V_DIM:(h + 1) * A_V_DIM] = (hv * jax.nn.sigmoid(o)).astype(BF16)

        b_last = b_c[L - 1:L, :]
        g_c = b_last - b_c + li_c
        g_r = b_last - b_r + li_r
        m_new = jnp.maximum(b_last + m_prev, jnp.max(g_r, axis=-1, keepdims=True))
        decay = jnp.exp(b_last + m_prev - m_new)
        kw = k * jnp.exp(g_c - m_new)
        c_ref[st] = decay * c_prev + _dot_tn(kw.astype(BF16), vb)
        n_ref[st] = decay * n_prev + jnp.sum(kw, axis=0, keepdims=True)
        m_ref[st] = m_new

    z_prev = None
    for b in range(NB):
        z, pieces = projection(b)
        for h in range(A_HEADS):
            if z_prev is not None:
                head(h, b - 1, z_prev)
            pieces[h]()
        z_prev = z
    for h in range(A_HEADS):
        head(h, NB - 1, z_prev)

    for b in range(NB):
        out_ref[b] = x_ref[b] + _dot(hcat_ref[b], wo_ref[...])


def _mlstm_layer(x, norm_g, w_in, b_if, hnorm_g, w_out):
    B, S, D = x.shape
    L = MLSTM_CHUNK
    wg = w_in[:, A_Z_DIM:]
    ng = 2 * A_HEADS
    wgc = jnp.pad(wg, ((0, 0), (0, GATE_LANES - ng))).astype(BF16)
    wgr = jnp.pad(wg.T, ((0, GATE_ROWS - ng), (0, 0))).astype(BF16)
    bc = jnp.pad(b_if[None, :], ((0, 0), (0, GATE_LANES - ng)))
    br = jnp.pad(b_if[:, None], ((0, GATE_ROWS - ng), (0, 0)))
    row_spec = pl.BlockSpec((B, L, D), lambda c: (0, c, 0))
    return pl.pallas_call(
        _mlstm_kernel,
        out_shape=jax.ShapeDtypeStruct((B, S, D), F32),
        grid=(S // L,),
        in_specs=[row_spec, _const_spec((1, D)), _const_spec(w_in.shape),
                  _const_spec((D, GATE_LANES)), _const_spec((GATE_ROWS, D)),
                  _const_spec((1, GATE_LANES)), _const_spec((GATE_ROWS, 1)),
                  _const_spec((1, A_NV)), _const_spec((A_NV, D))],
        out_specs=row_spec,
        scratch_shapes=[pltpu.VMEM((B * A_HEADS, A_QK_DIM, A_V_DIM), F32),
                        pltpu.VMEM((B * A_HEADS, 1, A_QK_DIM), F32),
                        pltpu.VMEM((B * A_HEADS, 1, 1), F32),
                        pltpu.VMEM((B, L, A_NV), BF16),
                        pltpu.VMEM((D, A_Z_DIM), BF16), pltpu.VMEM((A_NV, D), BF16)],
        compiler_params=pltpu.CompilerParams(
            dimension_semantics=("arbitrary",), vmem_limit_bytes=VMEM_LIMIT),
        name="mlstm_layer",
    )(x, norm_g[None, :], w_in, wgc, wgr, bc, br, hnorm_g.reshape(1, A_NV), w_out)


def _ffn_kernel(x_ref, *rest, final_layer):
    if final_layer:
        (branch_ref, g_ref, wup_ref, cw_ref, cb_ref, wdn_ref, fg_ref,
         out_ref, ubuf_ref, act_ref, slab_ref) = rest
    else:
        (g_ref, wup_ref, cw_ref, cb_ref, wdn_ref,
         out_ref, xh0_ref, xh1_ref, xh2_ref, ubuf_ref, act_ref, slab_ref, cls_ref) = rest
    T = x_ref.shape[0]
    H = CONV_HALO
    half = T // 2
    nslab = D_MODEL // LANES

    @pl.when(pl.program_id(1) == 0)
    def _():
        ubuf_ref[:, 0:H, :] = jnp.zeros((2 * D_FF // LANES, H, LANES), F32)

    x = x_ref[...]
    if final_layer:
        x = x + branch_ref[...].astype(F32)
    xn = (x * _rms_scale(x) * g_ref[...]).astype(BF16)
    per = FFN_COLS // LANES

    def up(c):
        for base in (0, D_FF):
            u = _dot(xn, wup_ref[:, base + c * FFN_COLS:base + (c + 1) * FFN_COLS])
            for s in range(per):
                ubuf_ref[base // LANES + c * per + s, H:H + T, :] = u[:, s * LANES:(s + 1) * LANES]

    def conv(k, parity):
        cs = slice(k * LANES, (k + 1) * LANES)
        taps = [ubuf_ref[k, pl.ds(H + parity - 2 + i, half, stride=2), :] for i in range(3)]
        return (taps[0] * cw_ref[0:1, cs] + taps[1] * cw_ref[1:2, cs] + taps[2] * cw_ref[2:3, cs]
                + cb_ref[:, cs])

    def activate(c):
        for j in range(c * per, (c + 1) * per):
            for parity in range(2):
                gate = conv(j, parity)
                val = conv(D_FF // LANES + j, parity)
                act_ref[parity * half:(parity + 1) * half, j * LANES:(j + 1) * LANES] = (
                    gate * jax.nn.sigmoid(gate) * val).astype(BF16)

    nchunk = D_FF // FFN_COLS
    up(0)
    for c in range(nchunk):
        if c + 1 < nchunk:
            up(c + 1)
        activate(c)

    ubuf_ref[:, 0:H, :] = ubuf_ref[:, T:T + H, :]
    down = _dot(act_ref[...], wdn_ref[...])
    for k in range(nslab):
        for parity in range(2):
            slab_ref[k, pl.ds(parity, half, stride=2), :] = (
                down[parity * half:(parity + 1) * half, k * LANES:(k + 1) * LANES])
    y = x + jnp.concatenate([slab_ref[k] for k in range(nslab)], axis=1)
    if final_layer:
        out_ref[...] = y * _rms_scale(y) * fg_ref[...]
        return
    out_ref[...] = y
    xh = y * _rms_scale(y)
    xh0_ref[...] = xh.astype(BF16)
    for k in range(nslab):
        slab_ref[k] = xh[:, k * LANES:(k + 1) * LANES]
    d1, d2 = B_GROUPS[1][1], B_GROUPS[2][1]
    sub = d2 // d1
    for r1 in range(d1):
        for k in range(nslab):
            rows = slab_ref[k, pl.ds(r1, T // d1, stride=d1), :]
            cls_ref[r1 * nslab + k] = rows
            c0 = r1 * D_MODEL + k * LANES
            xh1_ref[:, c0:c0 + LANES] = rows.astype(BF16)
    for r1 in range(d1):
        for q in range(sub):
            for k in range(nslab):
                c0 = (r1 + d1 * q) * D_MODEL + k * LANES
                xh2_ref[:, c0:c0 + LANES] = (
                    cls_ref[r1 * nslab + k, pl.ds(q, T // d2, stride=sub), :].astype(BF16))


def _conv_ffn(x, layer, norm_g, w_up, conv_w, conv_b, w_down, branch=None, final_g=None):
    B, S, D = x.shape
    T = FFN_ROWS
    final_layer = branch is not None
    row_spec = pl.BlockSpec((None, T, D), lambda b, t: (b, t, 0))

    def layer_spec(*shape):
        return pl.BlockSpec((None,) + shape, lambda *_: (layer,) + (0,) * len(shape),
                            pipeline_mode=pl.Buffered(1))

    weight_specs = [layer_spec(1, D), layer_spec(D, 2 * D_FF), layer_spec(3, 2 * D_FF),
                    layer_spec(1, 2 * D_FF), layer_spec(D_FF, D)]
    weights = [norm_g[:, None, :], w_up, conv_w, conv_b[:, None, :], w_down]
    out_shape = [jax.ShapeDtypeStruct((B, S, D), F32)]
    out_specs = [row_spec]
    scratch = [pltpu.VMEM((2 * D_FF // LANES, T + CONV_HALO, LANES), F32),
               pltpu.VMEM((T, D_FF), BF16), pltpu.VMEM((D // LANES, T, LANES), F32)]
    if final_layer:
        in_specs = [row_spec, row_spec] + weight_specs + [_const_spec((1, D))]
        operands = [x, branch] + weights + [final_g[None, :]]
    else:
        in_specs = [row_spec] + weight_specs
        operands = [x] + weights
        out_shape.append(jax.ShapeDtypeStruct((B, S, D), BF16))
        out_specs.append(row_spec)
        for _, dil in B_GROUPS[1:]:
            out_shape.append(jax.ShapeDtypeStruct((B, S // dil, dil * D), BF16))
            out_specs.append(pl.BlockSpec((None, T // dil, dil * D), lambda b, t: (b, t, 0)))
        d1 = B_GROUPS[1][1]
        scratch.append(pltpu.VMEM((d1 * D // LANES, T // d1, LANES), F32))
    res = pl.pallas_call(
        functools.partial(_ffn_kernel, final_layer=final_layer),
        out_shape=out_shape,
        grid=(B, S // T),
        in_specs=in_specs,
        out_specs=out_specs,
        scratch_shapes=scratch,
        compiler_params=pltpu.CompilerParams(
            dimension_semantics=("arbitrary", "arbitrary"), vmem_limit_bytes=VMEM_LIMIT),
        name="conv_ffn_final" if final_layer else "conv_ffn",
    )(*operands)
    return res[0] if final_layer else res


def _attn_kernel(xh_ref, wq_raw_ref, wk_raw_ref, wv_raw_ref, gq_ref, gkv_ref, gkv_row_ref, band_ref,
                 o_ref, lse_ref, q2_ref, kt_ref, v_ref, bias_ref, wq_ref, wkt_ref, wv_ref):
    T = xh_ref.shape[0]
    P = B_BLOCK
    i = pl.program_id(2)

    @pl.when((pl.program_id(0) == 0) & (pl.program_id(1) == 0) & (i == 0))
    def _():
        for c in range(D_MODEL // MXU_TILE):
            rs = slice(c * MXU_TILE, (c + 1) * MXU_TILE)
            wq_ref[rs, :] = (wq_raw_ref[rs, :] * gq_ref[rs, :]).astype(BF16)
            wv_ref[rs, :] = (wv_raw_ref[rs, :] * gkv_ref[rs, :]).astype(BF16)
            wkt_ref[:, rs] = (wk_raw_ref[rs, :].T * gkv_row_ref[:, rs]).astype(BF16)
        prev_key = lax.broadcasted_iota(jnp.int32, (P, 2 * P), 1) < P
        n = band_ref.shape[1]
        for h in range(B_HEADS):
            rows = jnp.broadcast_to(band_ref[h:h + 1, :], (P, n))
            table = pltpu.roll(rows, 0, 1, stride=1, stride_axis=0)[:, :2 * P]
            half = slice((h % 2) * P, (h % 2 + 1) * P)
            bias_ref[0, h // 2, half, :] = table
            bias_ref[1, h // 2, half, :] = jnp.where(prev_key, MASK_VALUE, table)

    @pl.when(i == 0)
    def _():
        kt_ref[:, 0:P] = jnp.zeros((D_MODEL, P), BF16)
        v_ref[0:P, :] = jnp.zeros((P, D_MODEL), BF16)

    @pl.when(i > 0)
    def _():
        kt_ref[:, 0:P] = kt_ref[:, T:T + P]
        v_ref[0:P, :] = v_ref[T:T + P, :]

    lane = lax.broadcasted_iota(jnp.int32, (P, D_MODEL), 1)
    even_head = (lane % (2 * B_HEAD_DIM)) < B_HEAD_DIM
    lane2 = lax.broadcasted_iota(jnp.int32, (P, 2 * B_HEAD_DIM), 1)
    first_half = lane2 < B_HEAD_DIM
    ones = jnp.ones((2 * P, LANES), BF16)
    first = (i == 0).astype(jnp.int32)

    xh = xh_ref[...]
    q = _dot(xh, wq_ref[...]) * (B_HEAD_DIM ** -0.5)
    for j in range(T // P):
        qj = q[j * P:(j + 1) * P]
        q2_ref[j, 0:P, :] = jnp.where(even_head, qj, 0.0).astype(BF16)
        q2_ref[j, P:2 * P, :] = jnp.where(even_head, 0.0, qj).astype(BF16)
    kt_ref[:, P:P + T] = _dot_nt(wkt_ref[...], xh).astype(BF16)
    v_ref[P:P + T, :] = _dot(xh, wv_ref[...]).astype(BF16)

    for j in range(T // P):
        lse_all = jnp.zeros((P, LANES), F32)
        for hp in range(B_HEADS // 2):
            cs = slice(hp * 2 * B_HEAD_DIM, (hp + 1) * 2 * B_HEAD_DIM)
            s = (_dot(q2_ref[j, :, cs], kt_ref[cs, j * P:(j + 2) * P])
                 + bias_ref[first if j == 0 else 0, hp])
            m = jnp.max(s, axis=-1, keepdims=True)
            p = jnp.exp(s - m).astype(BF16)
            va = jnp.concatenate([v_ref[j * P:(j + 2) * P, cs], ones], axis=1)
            pv = _dot(p, va)
            l = pv[:, LANES:]
            o = pv[:, :LANES] * (1.0 / l)
            lse = m + jnp.log(l)
            o_ref[j * P:(j + 1) * P, cs] = jnp.where(first_half, o[0:P], o[P:2 * P]).astype(BF16)
            lse_all = jnp.where(lane2 == 2 * hp, lse[0:P],
                                jnp.where(lane2 == 2 * hp + 1, lse[P:2 * P], lse_all))
        lse_ref[j * P:(j + 1) * P, :] = lse_all


def _t5_bucket(dist):
    max_exact = REL_BUCKETS // 2
    d = np.maximum(dist, 0)
    log_ratio = np.log(np.maximum(d, 1) / max_exact) / math.log(REL_MAX_DIST / max_exact)
    large = np.minimum(max_exact + (log_ratio * (REL_BUCKETS - max_exact)).astype(np.int64),
                       REL_BUCKETS - 1)
    return np.where(d < max_exact, d, large).astype(np.int32)


def _band_bias(rel_bias):
    P = B_BLOCK
    n = 3 * P
    t = np.arange(n)
    delta = P - np.where(t < 2 * P, t, t - n)
    vecs = []
    for g, (win, dil) in enumerate(B_GROUPS):
        valid = (delta >= 0) & (delta <= win // dil)
        vec = rel_bias[_t5_bucket(delta * dil)][:, g * B_HEADS:(g + 1) * B_HEADS].astype(F32)
        vecs.append(jnp.where(valid[:, None], vec, MASK_VALUE).T)
    return jnp.stack(vecs)


def _attn_group(xh, g, dil, w_q, w_kv, q_gain, kv_gain, bias_all):
    B, n, _ = xh.shape
    D = D_MODEL
    T = ATT_ROWS
    cls_spec = pl.BlockSpec((None, T, D), lambda b, r, i: (b, i, r))
    lse_spec = pl.BlockSpec((None, T, LANES), lambda b, r, i: (b, i, r))

    def weight_spec(row_blk, col_blk):
        return pl.BlockSpec((D, D), lambda *_: (row_blk, col_blk), pipeline_mode=pl.Buffered(1))

    return pl.pallas_call(
        _attn_kernel,
        out_shape=[jax.ShapeDtypeStruct((B, n, dil * D), BF16),
                   jax.ShapeDtypeStruct((B, n, dil * LANES), F32)],
        grid=(B, dil, n // T),
        in_specs=[cls_spec, weight_spec(0, g), weight_spec(0, g), weight_spec(0, len(B_GROUPS) + g),
                  _const_spec((D, 1)), _const_spec((D, 1)), _const_spec((1, D)),
                  pl.BlockSpec((None,) + bias_all.shape[1:], lambda *_: (g, 0, 0),
                               pipeline_mode=pl.Buffered(1))],
        out_specs=[cls_spec, lse_spec],
        scratch_shapes=[pltpu.VMEM((T // B_BLOCK, 2 * B_BLOCK, D), BF16),
                        pltpu.VMEM((D, T + B_BLOCK), BF16), pltpu.VMEM((T + B_BLOCK, D), BF16),
                        pltpu.VMEM((2, B_HEADS // 2, 2 * B_BLOCK, 2 * B_BLOCK), F32),
                        pltpu.VMEM((D, D), BF16), pltpu.VMEM((D, D), BF16), pltpu.VMEM((D, D), BF16)],
        compiler_params=pltpu.CompilerParams(
            dimension_semantics=("arbitrary", "arbitrary", "arbitrary"),
            vmem_limit_bytes=VMEM_LIMIT),
        name=f"dilated_attn_g{g}",
    )(xh, w_q, w_kv, w_kv, q_gain[:, None], kv_gain[:, None], kv_gain[None, :], bias_all)


def _merge_kernel(o0_ref, o1_ref, o2_ref, l0_ref, l1_ref, l2_ref, ex_ref, wo_ref, out_ref,
                  os1_ref, os2_ref, ls1_ref, ls2_ref, mg_ref, mid_ref):
    T = o0_ref.shape[0]
    nslab = D_MODEL // LANES
    d1, d2 = B_GROUPS[1][1], B_GROUPS[2][1]
    sub = d2 // d1
    for dil, l_src, l_dst in ((d1, l1_ref, ls1_ref), (d2, l2_ref, ls2_ref)):
        for r in range(dil):
            l_dst[pl.ds(r, T // dil, stride=dil), :] = l_src[:, r * LANES:(r + 1) * LANES]
    l0, l1, l2 = l0_ref[...], ls1_ref[...], ls2_ref[...]
    m = jnp.maximum(jnp.maximum(l0, l1), l2)
    e0, e1, e2 = jnp.exp(l0 - m), jnp.exp(l1 - m), jnp.exp(l2 - m)
    inv = 1.0 / (e0 + e1 + e2)

    def spread(w):
        return _dot(w.astype(BF16), ex_ref[...])

    w0, w1 = spread(e0 * inv), spread(e1 * inv)

    for k in range(nslab):
        for r1 in range(d1):
            c0 = r1 * D_MODEL + k * LANES
            os1_ref[k, pl.ds(r1, T // d1, stride=d1), :] = o1_ref[:, c0:c0 + LANES].astype(F32)
            for q in range(sub):
                c0 = (r1 + d1 * q) * D_MODEL + k * LANES
                mid_ref[k * d1 + r1, pl.ds(q, T // d2, stride=sub), :] = (
                    o2_ref[:, c0:c0 + LANES].astype(F32))
            os2_ref[k, pl.ds(r1, T // d1, stride=d1), :] = mid_ref[k * d1 + r1]
        ks = slice(k * LANES, (k + 1) * LANES)
        o2 = os2_ref[k]
        mg = o2 + w0[:, ks] * (o0_ref[:, ks].astype(F32) - o2) + w1[:, ks] * (os1_ref[k] - o2)
        mg_ref[:, ks] = mg.astype(BF16)
    out_ref[...] = _dot(mg_ref[...], wo_ref[...]).astype(BF16)


def _merge_groups(outs, lses, w_out):
    B, S, D = outs[0].shape
    T = MERGE_ROWS
    row_spec = pl.BlockSpec((None, T, D), lambda b, t: (b, t, 0))
    o_specs = [pl.BlockSpec((None, T // dil, dil * D), lambda b, t: (b, t, 0)) for _, dil in B_GROUPS]
    l_specs = [pl.BlockSpec((None, T // dil, dil * LANES), lambda b, t: (b, t, 0))
               for _, dil in B_GROUPS]
    expand = np.zeros((LANES, D), np.float32)
    for h in range(B_HEADS):
        expand[h, h * B_HEAD_DIM:(h + 1) * B_HEAD_DIM] = 1.0
    nslab = D // LANES
    return pl.pallas_call(
        _merge_kernel,
        out_shape=jax.ShapeDtypeStruct((B, S, D), BF16),
        grid=(B, S // T),
        in_specs=o_specs + l_specs + [_const_spec((LANES, D)), _const_spec((D, D))],
        out_specs=row_spec,
        scratch_shapes=[pltpu.VMEM((nslab, T, LANES), F32), pltpu.VMEM((nslab, T, LANES), F32),
                        pltpu.VMEM((T, LANES), F32), pltpu.VMEM((T, LANES), F32),
                        pltpu.VMEM((T, D), BF16),
                        pltpu.VMEM((nslab * B_GROUPS[1][1], T // B_GROUPS[1][1], LANES), F32)],
        compiler_params=pltpu.CompilerParams(
            dimension_semantics=("arbitrary", "arbitrary"), vmem_limit_bytes=VMEM_LIMIT),
        name="merge_groups",
    )(*outs, *lses, jnp.asarray(expand, BF16), w_out.astype(BF16))


def kernel(x, a_norm_g, a_w_in, a_b_if, a_hnorm_g, a_w_out, kv_norm_g, w_kv, b_norm_g, b_w_q,
           b_w_out, rel_bias, f_norm_g, f_w_up, f_conv_w, f_conv_b, f_w_down, final_norm_g):
    x = _mlstm_layer(x, a_norm_g[0], a_w_in[0], a_b_if[0], a_hnorm_g[0], a_w_out[0])
    ffn_params = (f_norm_g, f_w_up.astype(BF16), f_conv_w, f_conv_b, f_w_down.astype(BF16))
    x, *streams = _conv_ffn(x, 0, *ffn_params)
    bias_all = _band_bias(rel_bias)
    outs, lses = [], []
    for g, (_, dil) in enumerate(B_GROUPS):
        o, lse = _attn_group(streams[g], g, dil, b_w_q[0], w_kv, b_norm_g[0], kv_norm_g, bias_all)
        outs.append(o)
        lses.append(lse)
    branch = _merge_groups(outs, lses, b_w_out[0])
    return _conv_ffn(x, 1, *ffn_params, branch=branch, final_g=final_norm_g)
```

```python
import functools
import math

import numpy as np
import jax
import jax.numpy as jnp
from jax import lax
from jax.experimental import pallas as pl
from jax.experimental.pallas import tpu as pltpu

F32 = jnp.float32
BF16 = jnp.bfloat16

D_MODEL = 1024
A_HEADS = 4
A_QK_DIM = 128
A_V_DIM = 256
A_NQ = A_HEADS * A_QK_DIM
A_NV = A_HEADS * A_V_DIM
A_Z_DIM = 2 * A_NQ + 2 * A_NV
GATE_SOFTCAP = 15.0
B_GROUPS = ((128, 1), (512, 4), (2048, 16))
B_HEAD_DIM = 64
B_HEADS = 16
B_BLOCK = 128
REL_BUCKETS = 32
REL_MAX_DIST = 2048
D_FF = 2816
EPS = 1e-6
MASK_VALUE = -1e30

LANES = 128
SUBLANES = 8
BF16_ROWS = 2 * SUBLANES
MXU_TILE = 256
V7X_VMEM_BYTES = 64 * 1024 * 1024

GATE_LANES = LANES
GATE_ROWS = BF16_ROWS
MLSTM_CHUNK = 256
FFN_ROWS = 512
FFN_COLS = MXU_TILE
CONV_HALO = SUBLANES
ATT_ROWS = 1024
MERGE_ROWS = 1024
VMEM_LIMIT = V7X_VMEM_BYTES * 7 // 8


def _rms_scale(x):
    return lax.rsqrt(jnp.mean(x * x, axis=-1, keepdims=True) + EPS)


def _softcap(z):
    return GATE_SOFTCAP * jnp.tanh(z / GATE_SOFTCAP)


def _log_sigmoid(a):
    return jnp.minimum(a, 0.0) - jnp.log1p(jnp.exp(-jnp.abs(a)))


def _split3(v):
    hi = v.astype(BF16)
    r1 = v - hi.astype(F32)
    mid = r1.astype(BF16)
    lo = (r1 - mid.astype(F32)).astype(BF16)
    return hi, mid, lo


def _dot(a, b):
    return jnp.dot(a, b, preferred_element_type=F32)


def _dot_nt(a, b):
    return lax.dot_general(a, b, (((1,), (1,)), ((), ())), preferred_element_type=F32)


def _dot_tn(a, b):
    return lax.dot_general(a, b, (((0,), (0,)), ((), ())), preferred_element_type=F32)


def _const_spec(shape):
    return pl.BlockSpec(shape, lambda *_: (0,) * len(shape), pipeline_mode=pl.Buffered(1))


def _mlstm_kernel(x_ref, g_ref, win_ref, wgc_ref, wgr_ref, bc_ref, br_ref, hg_ref, wout_ref,
                  out_ref, c_ref, n_ref, m_ref, hcat_ref, wz_ref, wo_ref):
    NB, L = x_ref.shape[0], x_ref.shape[1]

    @pl.when(pl.program_id(0) == 0)
    def _():
        c_ref[...] = jnp.zeros_like(c_ref)
        n_ref[...] = jnp.zeros_like(n_ref)
        m_ref[...] = jnp.zeros_like(m_ref)
        for c in range(D_MODEL // MXU_TILE):
            rs = slice(c * MXU_TILE, (c + 1) * MXU_TILE)
            wz_ref[rs, :] = win_ref[rs, 0:A_Z_DIM].astype(BF16)
            wo_ref[rs, :] = wout_ref[rs, :].astype(BF16)

    row = lax.broadcasted_iota(jnp.int32, (L, L), 0)
    col = lax.broadcasted_iota(jnp.int32, (L, L), 1)
    causal = col <= row
    tril = jnp.where(causal, 1.0, 0.0).astype(BF16)
    triu = jnp.where(row <= col, 1.0, 0.0).astype(BF16)

    def projection(b):
        x = x_ref[b]
        xn = (x * _rms_scale(x) * g_ref[...]).astype(BF16)
        z = {}

        def gates():
            z["ac"] = _softcap(_dot(xn, wgc_ref[...]) + bc_ref[...])
            z["ar"] = _softcap(_dot_nt(wgr_ref[...], xn) + br_ref[...])
            z["bcs"] = sum(_dot(tril, p) for p in _split3(_log_sigmoid(z["ac"])))
            z["brs"] = sum(_dot(p, triu) for p in _split3(_log_sigmoid(z["ar"])))

        def columns(name, c0, c1):
            z[name] = _dot(xn, wz_ref[:, c0:c1])

        return z, [gates,
                   functools.partial(columns, "qk", 0, 2 * A_NQ),
                   functools.partial(columns, "v", 2 * A_NQ, 2 * A_NQ + A_NV),
                   functools.partial(columns, "o", 2 * A_NQ + A_NV, A_Z_DIM)]

    def head(h, b, z):
        st = b * A_HEADS + h
        q = z["qk"][:, h * A_QK_DIM:(h + 1) * A_QK_DIM] * (A_QK_DIM ** -0.5)
        k = z["qk"][:, A_NQ + h * A_QK_DIM:A_NQ + (h + 1) * A_QK_DIM]
        v = z["v"][:, h * A_V_DIM:(h + 1) * A_V_DIM]
        o = z["o"][:, h * A_V_DIM:(h + 1) * A_V_DIM]
        qb, kb, vb = q.astype(BF16), k.astype(BF16), v.astype(BF16)
        li_c = z["ac"][:, h:h + 1]
        li_r = z["ar"][h:h + 1, :]
        b_c = z["bcs"][:, A_HEADS + h:A_HEADS + h + 1]
        b_r = z["brs"][A_HEADS + h:A_HEADS + h + 1, :]
        m_prev = m_ref[st]
        c_prev = c_ref[st]
        n_prev = n_ref[st]

        log_d = jnp.where(causal, b_c - b_r + li_r, -jnp.inf)
        m_inter = b_c + m_prev
        m_t = jnp.maximum(m_inter, jnp.max(log_d, axis=-1, keepdims=True))
        sm = _dot_nt(qb, kb) * jnp.exp(log_d - m_t)
        w_inter = jnp.exp(m_inter - m_t)
        num = _dot(sm.astype(BF16), vb) + w_inter * _dot(qb, c_prev.astype(BF16))
        den = (jnp.sum(sm, axis=-1, keepdims=True)
               + w_inter * jnp.sum(q * n_prev, axis=-1, keepdims=True))
        hv = num * (1.0 / jnp.maximum(jnp.abs(den), jnp.exp(-m_t)))
        hv = hv * _rms_scale(hv) * hg_ref[:, h * A_V_DIM:(h + 1) * A_V_DIM]
        hcat_ref[b, :, h * A_V_DIM:(h + 1) * A_V_DIM] = (hv * jax.nn.sigmoid(o)).astype(BF16)

        b_last = b_c[L - 1:L, :]
        g_c = b_last - b_c + li_c
        g_r = b_last - b_r + li_r
        m_new = jnp.maximum(b_last + m_prev, jnp.max(g_r, axis=-1, keepdims=True))
        decay = jnp.exp(b_last + m_prev - m_new)
        kw = k * jnp.exp(g_c - m_new)
        c_ref[st] = decay * c_prev + _dot_tn(kw.astype(BF16), vb)
        n_ref[st] = decay * n_prev + jnp.sum(kw, axis=0, keepdims=True)
        m_ref[st] = m_new

    z_prev = None
    for b in range(NB):
        z, pieces = projection(b)
        for h in range(A_HEADS):
            if z_prev is not None:
                head(h, b - 1, z_prev)
            pieces[h]()
        z_prev = z
    for h in range(A_HEADS):
        head(h, NB - 1, z_prev)

    for b in range(NB):
        out_ref[b] = x_ref[b] + _dot(hcat_ref[b], wo_ref[...])


def _mlstm_layer(x, norm_g, w_in, b_if, hnorm_g, w_out):
    B, S, D = x.shape
    L = MLSTM_CHUNK
    wg = w_in[:, A_Z_DIM:]
    ng = 2 * A_HEADS
    wgc = jnp.pad(wg, ((0, 0), (0, GATE_LANES - ng))).astype(BF16)
    wgr = jnp.pad(wg.T, ((0, GATE_ROWS - ng), (0, 0))).astype(BF16)
    bc = jnp.pad(b_if[None, :], ((0, 0), (0, GATE_LANES - ng)))
    br = jnp.pad(b_if[:, None], ((0, GATE_ROWS - ng), (0, 0)))
    row_spec = pl.BlockSpec((B, L, D), lambda c: (0, c, 0))
    return pl.pallas_call(
        _mlstm_kernel,
        out_shape=jax.ShapeDtypeStruct((B, S, D), F32),
        grid=(S // L,),
        in_specs=[row_spec, _const_spec((1, D)), _const_spec(w_in.shape),
                  _const_spec((D, GATE_LANES)), _const_spec((GATE_ROWS, D)),
                  _const_spec((1, GATE_LANES)), _const_spec((GATE_ROWS, 1)),
                  _const_spec((1, A_NV)), _const_spec((A_NV, D))],
        out_specs=row_spec,
        scratch_shapes=[pltpu.VMEM((B * A_HEADS, A_QK_DIM, A_V_DIM), F32),
                        pltpu.VMEM((B * A_HEADS, 1, A_QK_DIM), F32),
                        pltpu.VMEM((B * A_HEADS, 1, 1), F32),
                        pltpu.VMEM((B, L, A_NV), BF16),
                        pltpu.VMEM((D, A_Z_DIM), BF16), pltpu.VMEM((A_NV, D), BF16)],
        compiler_params=pltpu.CompilerParams(
            dimension_semantics=("arbitrary",), vmem_limit_bytes=VMEM_LIMIT),
        name="mlstm_layer",
    )(x, norm_g[None, :], w_in, wgc, wgr, bc, br, hnorm_g.reshape(1, A_NV), w_out)


def _ffn_kernel(x_ref, *rest, final_layer):
    if final_layer:
        (branch_ref, g_ref, wup_ref, cw_ref, cb_ref, wdn_ref, fg_ref,
         out_ref, ubuf_ref, act_ref, slab_ref) = rest
    else:
        (g_ref, wup_ref, cw_ref, cb_ref, wdn_ref,
         out_ref, xh0_ref, xh1_ref, xh2_ref, ubuf_ref, act_ref, slab_ref, cls_ref) = rest
    T = x_ref.shape[0]
    H = CONV_HALO
    half = T // 2
    nslab = D_MODEL // LANES

    @pl.when(pl.program_id(1) == 0)
    def _():
        ubuf_ref[:, 0:H, :] = jnp.zeros((2 * D_FF // LANES, H, LANES), F32)

    x = x_ref[...]
    if final_layer:
        x = x + branch_ref[...].astype(F32)
    xn = (x * _rms_scale(x) * g_ref[...]).astype(BF16)
    per = FFN_COLS // LANES

    def up(c):
        for base in (0, D_FF):
            u = _dot(xn, wup_ref[:, base + c * FFN_COLS:base + (c + 1) * FFN_COLS])
            for s in range(per):
                ubuf_ref[base // LANES + c * per + s, H:H + T, :] = u[:, s * LANES:(s + 1) * LANES]

    def conv(k, parity):
        cs = slice(k * LANES, (k + 1) * LANES)
        taps = [ubuf_ref[k, pl.ds(H + parity - 2 + i, half, stride=2), :] for i in range(3)]
        return (taps[0] * cw_ref[0:1, cs] + taps[1] * cw_ref[1:2, cs] + taps[2] * cw_ref[2:3, cs]
                + cb_ref[:, cs])

    def activate(c):
        for j in range(c * per, (c + 1) * per):
            for parity in range(2):
                gate = conv(j, parity)
                val = conv(D_FF // LANES + j, parity)
                act_ref[parity * half:(parity + 1) * half, j * LANES:(j + 1) * LANES] = (
                    gate * jax.nn.sigmoid(gate) * val).astype(BF16)

    nchunk = D_FF // FFN_COLS
    up(0)
    for c in range(nchunk):
        if c + 1 < nchunk:
            up(c + 1)
        activate(c)

    ubuf_ref[:, 0:H, :] = ubuf_ref[:, T:T + H, :]
    down = _dot(act_ref[...], wdn_ref[...])
    for k in range(nslab):
        for parity in range(2):
            slab_ref[k, pl.ds(parity, half, stride=2), :] = (
                down[parity * half:(parity + 1) * half, k * LANES:(k + 1) * LANES])
    y = x + jnp.concatenate([slab_ref[k] for k in range(nslab)], axis=1)
    if final_layer:
        out_ref[...] = y * _rms_scale(y) * fg_ref[...]
        return
    out_ref[...] = y
    xh = y * _rms_scale(y)
    xh0_ref[...] = xh.astype(BF16)
    for k in range(nslab):
        slab_ref[k] = xh[:, k * LANES:(k + 1) * LANES]
    d1, d2 = B_GROUPS[1][1], B_GROUPS[2][1]
    sub = d2 // d1
    for r1 in range(d1):
        for k in range(nslab):
            rows = slab_ref[k, pl.ds(r1, T // d1, stride=d1), :]
            cls_ref[r1 * nslab + k] = rows
            c0 = r1 * D_MODEL + k * LANES
            xh1_ref[:, c0:c0 + LANES] = rows.astype(BF16)
    for r1 in range(d1):
        for q in range(sub):
            for k in range(nslab):
                c0 = (r1 + d1 * q) * D_MODEL + k * LANES
                xh2_ref[:, c0:c0 + LANES] = (
                    cls_ref[r1 * nslab + k, pl.ds(q, T // d2, stride=sub), :].astype(BF16))


def _conv_ffn(x, layer, norm_g, w_up, conv_w, conv_b, w_down, branch=None, final_g=None):
    B, S, D = x.shape
    T = FFN_ROWS
    final_layer = branch is not None
    row_spec = pl.BlockSpec((None, T, D), lambda b, t: (b, t, 0))

    def layer_spec(*shape):
        return pl.BlockSpec((None,) + shape, lambda *_: (layer,) + (0,) * len(shape),
                            pipeline_mode=pl.Buffered(1))

    weight_specs = [layer_spec(1, D), layer_spec(D, 2 * D_FF), layer_spec(3, 2 * D_FF),
                    layer_spec(1, 2 * D_FF), layer_spec(D_FF, D)]
    weights = [norm_g[:, None, :], w_up, conv_w, conv_b[:, None, :], w_down]
    out_shape = [jax.ShapeDtypeStruct((B, S, D), F32)]
    out_specs = [row_spec]
    scratch = [pltpu.VMEM((2 * D_FF // LANES, T + CONV_HALO, LANES), F32),
               pltpu.VMEM((T, D_FF), BF16), pltpu.VMEM((D // LANES, T, LANES), F32)]
    if final_layer:
        in_specs = [row_spec, row_spec] + weight_specs + [_const_spec((1, D))]
        operands = [x, branch] + weights + [final_g[None, :]]
    else:
        in_specs = [row_spec] + weight_specs
        operands = [x] + weights
        out_shape.append(jax.ShapeDtypeStruct((B, S, D), BF16))
        out_specs.append(row_spec)
        for _, dil in B_GROUPS[1:]:
            out_shape.append(jax.ShapeDtypeStruct((B, S // dil, dil * D), BF16))
            out_specs.append(pl.BlockSpec((None, T // dil, dil * D), lambda b, t: (b, t, 0)))
        d1 = B_GROUPS[1][1]
        scratch.append(pltpu.VMEM((d1 * D // LANES, T // d1, LANES), F32))
    res = pl.pallas_call(
        functools.partial(_ffn_kernel, final_layer=final_layer),
        out_shape=out_shape,
        grid=(B, S // T),
        in_specs=in_specs,
        out_specs=out_specs,
        scratch_shapes=scratch,
        compiler_params=pltpu.CompilerParams(
            dimension_semantics=("arbitrary", "arbitrary"), vmem_limit_bytes=VMEM_LIMIT,
            allow_input_fusion=[op is w_up or op is w_down for op in operands]),
        name="conv_ffn_final" if final_layer else "conv_ffn",
    )(*operands)
    return res[0] if final_layer else res


def _attn_kernel(xh_ref, wq_raw_ref, wk_raw_ref, wv_raw_ref, gq_ref, gkv_ref, gkv_row_ref, band_ref,
                 o_ref, lse_ref, q2_ref, kt_ref, v_ref, bias_ref, wq_ref, wkt_ref, wv_ref):
    T = xh_ref.shape[0]
    P = B_BLOCK
    i = pl.program_id(2)

    @pl.when((pl.program_id(0) == 0) & (pl.program_id(1) == 0) & (i == 0))
    def _():
        for c in range(D_MODEL // MXU_TILE):
            rs = slice(c * MXU_TILE, (c + 1) * MXU_TILE)
            wq_ref[rs, :] = (wq_raw_ref[rs, :] * gq_ref[rs, :]).astype(BF16)
            wv_ref[rs, :] = (wv_raw_ref[rs, :] * gkv_ref[rs, :]).astype(BF16)
            wkt_ref[:, rs] = (wk_raw_ref[rs, :].T * gkv_row_ref[:, rs]).astype(BF16)
        prev_key = lax.broadcasted_iota(jnp.int32, (P, 2 * P), 1) < P
        n = band_ref.shape[1]
        for h in range(B_HEADS):
            rows = jnp.broadcast_to(band_ref[h:h + 1, :], (P, n))
            table = pltpu.roll(rows, 0, 1, stride=1, stride_axis=0)[:, :2 * P]
            half = slice((h % 2) * P, (h % 2 + 1) * P)
            bias_ref[0, h // 2, half, :] = table
            bias_ref[1, h // 2, half, :] = jnp.where(prev_key, MASK_VALUE, table)

    @pl.when(i == 0)
    def _():
        kt_ref[:, 0:P] = jnp.zeros((D_MODEL, P), BF16)
        v_ref[0:P, :] = jnp.zeros((P, D_MODEL), BF16)

    @pl.when(i > 0)
    def _():
        kt_ref[:, 0:P] = kt_ref[:, T:T + P]
        v_ref[0:P, :] = v_ref[T:T + P, :]

    lane = lax.broadcasted_iota(jnp.int32, (P, D_MODEL), 1)
    even_head = (lane % (2 * B_HEAD_DIM)) < B_HEAD_DIM
    lane2 = lax.broadcasted_iota(jnp.int32, (P, 2 * B_HEAD_DIM), 1)
    first_half = lane2 < B_HEAD_DIM
    ones = jnp.ones((2 * P, LANES), BF16)
    first = (i == 0).astype(jnp.int32)

    xh = xh_ref[...]
    q = _dot(xh, wq_ref[...]) * (B_HEAD_DIM ** -0.5)
    for j in range(T // P):
        qj = q[j * P:(j + 1) * P]
        q2_ref[j, 0:P, :] = jnp.where(even_head, qj, 0.0).astype(BF16)
        q2_ref[j, P:2 * P, :] = jnp.where(even_head, 0.0, qj).astype(BF16)
    kt_ref[:, P:P + T] = _dot_nt(wkt_ref[...], xh).astype(BF16)
    v_ref[P:P + T, :] = _dot(xh, wv_ref[...]).astype(BF16)

    for j in range(T // P):
        lse_all = jnp.zeros((P, LANES), F32)
        for hp in range(B_HEADS // 2):
            cs = slice(hp * 2 * B_HEAD_DIM, (hp + 1) * 2 * B_HEAD_DIM)
            s = (_dot(q2_ref[j, :, cs], kt_ref[cs, j * P:(j + 2) * P])
                 + bias_ref[first if j == 0 else 0, hp])
            m = jnp.max(s, axis=-1, keepdims=True)
            p = jnp.exp(s - m).astype(BF16)
            va = jnp.concatenate([v_ref[j * P:(j + 2) * P, cs], ones], axis=1)
            pv = _dot(p, va)
            l = pv[:, LANES:]
            o = pv[:, :LANES] * (1.0 / l)
            lse = m + jnp.log(l)
            o_ref[j * P:(j + 1) * P, cs] = jnp.where(first_half, o[0:P], o[P:2 * P]).astype(BF16)
            lse_all = jnp.where(lane2 == 2 * hp, lse[0:P],
                                jnp.where(lane2 == 2 * hp + 1, lse[P:2 * P], lse_all))
        lse_ref[j * P:(j + 1) * P, :] = lse_all


def _t5_bucket(dist):
    max_exact = REL_BUCKETS // 2
    d = np.maximum(dist, 0)
    log_ratio = np.log(np.maximum(d, 1) / max_exact) / math.log(REL_MAX_DIST / max_exact)
    large = np.minimum(max_exact + (log_ratio * (REL_BUCKETS - max_exact)).astype(np.int64),
                       REL_BUCKETS - 1)
    return np.where(d < max_exact, d, large).astype(np.int32)


def _band_bias(rel_bias):
    P = B_BLOCK
    n = 3 * P
    t = np.arange(n)
    delta = P - np.where(t < 2 * P, t, t - n)
    vecs = []
    for g, (win, dil) in enumerate(B_GROUPS):
        valid = (delta >= 0) & (delta <= win // dil)
        vec = rel_bias[_t5_bucket(delta * dil)][:, g * B_HEADS:(g + 1) * B_HEADS].astype(F32)
        vecs.append(jnp.where(valid[:, None], vec, MASK_VALUE).T)
    return jnp.stack(vecs)


def _attn_group(xh, g, dil, w_q, w_kv, q_gain, kv_gain, bias_all):
    B, n, _ = xh.shape
    D = D_MODEL
    T = ATT_ROWS
    cls_spec = pl.BlockSpec((None, T, D), lambda b, r, i: (b, i, r))
    lse_spec = pl.BlockSpec((None, T, LANES), lambda b, r, i: (b, i, r))

    def weight_spec(row_blk, col_blk):
        return pl.BlockSpec((D, D), lambda *_: (row_blk, col_blk), pipeline_mode=pl.Buffered(1))

    return pl.pallas_call(
        _attn_kernel,
        out_shape=[jax.ShapeDtypeStruct((B, n, dil * D), BF16),
                   jax.ShapeDtypeStruct((B, n, dil * LANES), F32)],
        grid=(B, dil, n // T),
        in_specs=[cls_spec, weight_spec(0, g), weight_spec(0, g), weight_spec(0, len(B_GROUPS) + g),
                  _const_spec((D, 1)), _const_spec((D, 1)), _const_spec((1, D)),
                  pl.BlockSpec((None,) + bias_all.shape[1:], lambda *_: (g, 0, 0),
                               pipeline_mode=pl.Buffered(1))],
        out_specs=[cls_spec, lse_spec],
        scratch_shapes=[pltpu.VMEM((T // B_BLOCK, 2 * B_BLOCK, D), BF16),
                        pltpu.VMEM((D, T + B_BLOCK), BF16), pltpu.VMEM((T + B_BLOCK, D), BF16),
                        pltpu.VMEM((2, B_HEADS // 2, 2 * B_BLOCK, 2 * B_BLOCK), F32),
                        pltpu.VMEM((D, D), BF16), pltpu.VMEM((D, D), BF16), pltpu.VMEM((D, D), BF16)],
        compiler_params=pltpu.CompilerParams(
            dimension_semantics=("arbitrary", "arbitrary", "arbitrary"),
            vmem_limit_bytes=VMEM_LIMIT),
        name=f"dilated_attn_g{g}",
    )(xh, w_q, w_kv, w_kv, q_gain[:, None], kv_gain[:, None], kv_gain[None, :], bias_all)


def _merge_kernel(o0_ref, o1_ref, o2_ref, l0_ref, l1_ref, l2_ref, ex_ref, wo_ref, out_ref,
                  os1_ref, os2_ref, ls1_ref, ls2_ref, mg_ref, mid_ref):
    T = o0_ref.shape[0]
    nslab = D_MODEL // LANES
    d1, d2 = B_GROUPS[1][1], B_GROUPS[2][1]
    sub = d2 // d1
    for dil, l_src, l_dst in ((d1, l1_ref, ls1_ref), (d2, l2_ref, ls2_ref)):
        for r in range(dil):
            l_dst[pl.ds(r, T // dil, stride=dil), :] = l_src[:, r * LANES:(r + 1) * LANES]
    l0, l1, l2 = l0_ref[...], ls1_ref[...], ls2_ref[...]
    m = jnp.maximum(jnp.maximum(l0, l1), l2)
    e0, e1, e2 = jnp.exp(l0 - m), jnp.exp(l1 - m), jnp.exp(l2 - m)
    inv = 1.0 / (e0 + e1 + e2)

    def spread(w):
        return _dot(w.astype(BF16), ex_ref[...])

    w0, w1 = spread(e0 * inv), spread(e1 * inv)

    for k in range(nslab):
        for r1 in range(d1):
            c0 = r1 * D_MODEL + k * LANES
            os1_ref[k, pl.ds(r1, T // d1, stride=d1), :] = o1_ref[:, c0:c0 + LANES].astype(F32)
            for q in range(sub):
                c0 = (r1 + d1 * q) * D_MODEL + k * LANES
                mid_ref[k * d1 + r1, pl.ds(q, T // d2, stride=sub), :] = (
                    o2_ref[:, c0:c0 + LANES].astype(F32))
            os2_ref[k, pl.ds(r1, T // d1, stride=d1), :] = mid_ref[k * d1 + r1]
        ks = slice(k * LANES, (k + 1) * LANES)
        o2 = os2_ref[k]
        mg = o2 + w0[:, ks] * (o0_ref[:, ks].astype(F32) - o2) + w1[:, ks] * (os1_ref[k] - o2)
        mg_ref[:, ks] = mg.astype(BF16)
    out_ref[...] = _dot(mg_ref[...], wo_ref[...]).astype(BF16)


def _merge_groups(outs, lses, w_out):
    B, S, D = outs[0].shape
    T = MERGE_ROWS
    row_spec = pl.BlockSpec((None, T, D), lambda b, t: (b, t, 0))
    o_specs = [pl.BlockSpec((None, T // dil, dil * D), lambda b, t: (b, t, 0)) for _, dil in B_GROUPS]
    l_specs = [pl.BlockSpec((None, T // dil, dil * LANES), lambda b, t: (b, t, 0))
               for _, dil in B_GROUPS]
    expand = np.zeros((LANES, D), np.float32)
    for h in range(B_HEADS):
        expand[h, h * B_HEAD_DIM:(h + 1) * B_HEAD_DIM] = 1.0
    nslab = D // LANES
    return pl.pallas_call(
        _merge_kernel,
        out_shape=jax.ShapeDtypeStruct((B, S, D), BF16),
        grid=(B, S // T),
        in_specs=o_specs + l_specs + [_const_spec((LANES, D)), _const_spec((D, D))],
        out_specs=row_spec,
        scratch_shapes=[pltpu.VMEM((nslab, T, LANES), F32), pltpu.VMEM((nslab, T, LANES), F32),
                        pltpu.VMEM((T, LANES), F32), pltpu.VMEM((T, LANES), F32),
                        pltpu.VMEM((T, D), BF16),
                        pltpu.VMEM((nslab * B_GROUPS[1][1], T // B_GROUPS[1][1], LANES), F32)],
        compiler_params=pltpu.CompilerParams(
            dimension_semantics=("arbitrary", "arbitrary"), vmem_limit_bytes=VMEM_LIMIT),
        name="merge_groups",
    )(*outs, *lses, jnp.asarray(expand, BF16), w_out.astype(BF16))


def kernel(x, a_norm_g, a_w_in, a_b_if, a_hnorm_g, a_w_out, kv_norm_g, w_kv, b_norm_g, b_w_q,
           b_w_out, rel_bias, f_norm_g, f_w_up, f_conv_w, f_conv_b, f_w_down, final_norm_g):
    x = _mlstm_layer(x, a_norm_g[0], a_w_in[0], a_b_if[0], a_hnorm_g[0], a_w_out[0])
    ffn_params = (f_norm_g, f_w_up.astype(BF16), f_conv_w, f_conv_b, f_w_down.astype(BF16))
    x, *streams = _conv_ffn(x, 0, *ffn_params)
    bias_all = _band_bias(rel_bias)
    outs, lses = [], []
    for g, (_, dil) in enumerate(B_GROUPS):
        o, lse = _attn_group(streams[g], g, dil, b_w_q[0], w_kv, b_norm_g[0], kv_norm_g, bias_all)
        outs.append(o)
        lses.append(lse)
    branch = _merge_groups(outs, lses, b_w_out[0])
    return _conv_ffn(x, 1, *ffn_params, branch=branch, final_g=final_norm_g)
```
